```python
import math
import jax, jax.numpy as jnp
from jax import lax
import numpy as np

D_MODEL = 2048
BATCH = 1
SEQ = 16384
DEPTH = 4

N_META = 16
BLK = 128
WINDOW = 128
HA = 8
DHA = 64
HBQ = 16
HBKV = 2
DHB = 64
G_B = HBQ // HBKV
D_FF = 4 * D_MODEL
N_BUCKETS = 32
MAX_DIST = 128
EPS = 1e-6
NEG = -1e30

QA_W = HA * 2 * DHA
KA_W = HA * 2 * DHA
VA_W = HA * 2 * DHA
QB_W = HBQ * DHB
KB_W = HBKV * DHB
VB_W = HBKV * DHB
WA = VA_W
WB = QB_W
IN_W = QA_W + KA_W + VA_W + QB_W + KB_W + VB_W + 2 * D_MODEL
SPLITS = list(np.cumsum([QA_W, KA_W, VA_W, QB_W, KB_W, VB_W, D_MODEL]).tolist())
PAD_FRONT = BLK - N_META

kernel_name = "hybrid_diffattn_swa_sink_gated_block"


def rmsnorm(x, g):
    xf = x.astype(jnp.float32)
    y = xf * lax.rsqrt(jnp.mean(xf * xf, axis=-1, keepdims=True) + EPS)
    return (y * g.astype(jnp.float32)).astype(x.dtype)


def t5_bucket(dist):
    n = jnp.maximum(dist, 0)
    max_exact = N_BUCKETS // 2
    nf = jnp.maximum(n, 1).astype(jnp.float32)
    large = max_exact + (jnp.log(nf / max_exact) / math.log(MAX_DIST / max_exact)
                         * (N_BUCKETS - max_exact)).astype(jnp.int32)
    large = jnp.minimum(large, N_BUCKETS - 1)
    return jnp.where(n < max_exact, n, large)


def diff_attention(q, k, v, rel_a, lam, lam_init, sub_g):
    B, P = q.shape[0], q.shape[1]
    nb = P // BLK
    pos = jnp.arange(P) - PAD_FRONT
    key_ok = pos >= 0
    qb = q.reshape(B, nb, BLK, HA, 2, DHA).transpose(1, 0, 2, 3, 4, 5)
    scale = DHA ** -0.5

    def one_block(args):
        q_blk, n = args
        qpos = n * BLK + jnp.arange(BLK) - PAD_FRONT
        dist = qpos[:, None] - pos[None, :]
        bias = rel_a[t5_bucket(dist)].astype(jnp.float32)
        mask = (dist >= 0) & key_ok[None, :]
        s = jnp.einsum('bqhcd,bkhcd->bhcqk', q_blk, k).astype(jnp.float32) * scale
        s = s + bias.transpose(2, 0, 1)[None, :, None]
        s = jnp.where(mask, s, NEG)
        p = jax.nn.softmax(s, axis=-1)
        a = p[:, :, 0] - lam * p[:, :, 1]
        return jnp.einsum('bhqk,bkhv->bqhv', a.astype(v.dtype), v)

    o = lax.map(one_block, (qb, jnp.arange(nb)))
    o = o.transpose(1, 0, 2, 3, 4).reshape(B, P, HA, 2 * DHA)
    o = rmsnorm(o, sub_g) * (1.0 - lam_init)
    return o.reshape(B, P, HA * 2 * DHA)


def swa_sink_attention(q, k, v, rel_b, sinks):
    B, P = q.shape[0], q.shape[1]
    nb = P // BLK
    scale = DHB ** -0.5
    qb = q.reshape(B, nb, BLK, HBKV, G_B, DHB)

    def band(t):
        tb = t.reshape(B, nb, BLK, HBKV, DHB)
        prev = jnp.concatenate([jnp.zeros_like(tb[:, :1]), tb[:, :-1]], axis=1)
        return jnp.concatenate([prev, tb], axis=2)

    kb, vb = band(k), band(v)
    km = k[:, PAD_FRONT:BLK]
    vm = v[:, PAD_FRONT:BLK]

    i = jnp.arange(BLK)
    j = jnp.arange(2 * BLK)
    m = jnp.arange(N_META)
    blocks = jnp.arange(nb)
    qpos = blocks[:, None] * BLK + i[None, :] - PAD_FRONT
    dist = BLK + i[:, None] - j[None, :]
    kidx = blocks[:, None] * BLK - BLK + j[None, :]
    band_mask = ((dist >= 0) & (dist < WINDOW))[None] & (kidx >= BLK)[:, None, :]
    dist_m = qpos[:, :, None] - m[None, None, :]
    meta_mask = dist_m >= 0

    bias_band = rel_b[t5_bucket(dist)].astype(jnp.float32)
    bias_band = bias_band.reshape(BLK, 2 * BLK, HBKV, G_B).transpose(2, 3, 0, 1)
    bias_meta = rel_b[t5_bucket(dist_m)].astype(jnp.float32)
    bias_meta = bias_meta.reshape(nb, BLK, N_META, HBKV, G_B).transpose(0, 3, 4, 1, 2)

    s_band = jnp.einsum('bnqhgd,bnkhd->bnhgqk', qb, kb).astype(jnp.float32) * scale + bias_band[None, None]
    s_band = jnp.where(band_mask[None, :, None, None], s_band, NEG)
    s_meta = jnp.einsum('bnqhgd,bmhd->bnhgqm', qb, km).astype(jnp.float32) * scale + bias_meta[None]
    s_meta = jnp.where(meta_mask[None, :, None, None], s_meta, NEG)
    sink = jnp.broadcast_to(sinks.astype(jnp.float32).reshape(HBKV, G_B)[None, None, :, :, None, None],
                            s_band.shape[:-1] + (1,))
    p = jax.nn.softmax(jnp.concatenate([s_meta, s_band, sink], axis=-1), axis=-1)
    p_meta = p[..., :N_META].astype(v.dtype)
    p_band = p[..., N_META:N_META + 2 * BLK].astype(v.dtype)
    o = (jnp.einsum('bnhgqm,bmhd->bnqhgd', p_meta, vm)
         + jnp.einsum('bnhgqk,bnkhd->bnqhgd', p_band, vb))
    return o.reshape(B, P, HBQ * DHB)


def setup_inputs(seed: int = 0) -> dict:
    key = jax.random.key(seed)
    ks = jax.random.split(key, 20)
    f32 = jnp.float32

    def nrm(k, shape, scale):
        return jax.random.normal(k, shape, f32) * scale

    def gain(k, shape):
        return 1.0 + 0.05 * jax.random.normal(k, shape, f32)

    return {
        "x": nrm(ks[0], (BATCH, SEQ, D_MODEL), 1.0),
        "meta_tokens": nrm(ks[1], (N_META, D_MODEL), 1.0),
        "rel_bias": nrm(ks[2], (N_BUCKETS, HA + HBQ), 0.5),
        "w_in": nrm(ks[3], (DEPTH, D_MODEL, IN_W), D_MODEL ** -0.5),
        "w_branch_a": nrm(ks[4], (DEPTH, WA, D_MODEL), WA ** -0.5),
        "w_branch_b": nrm(ks[5], (DEPTH, WB, D_MODEL), WB ** -0.5),
        "w_out": nrm(ks[6], (DEPTH, D_MODEL, D_MODEL), D_MODEL ** -0.5),
        "lambda_q1": nrm(ks[7], (DEPTH, DHA), 0.1),
        "lambda_k1": nrm(ks[8], (DEPTH, DHA), 0.1),
        "lambda_q2": nrm(ks[9], (DEPTH, DHA), 0.1),
        "lambda_k2": nrm(ks[10], (DEPTH, DHA), 0.1),
        "diff_norm": gain(ks[11], (DEPTH, 2 * DHA)),
        "sinks": nrm(ks[12], (DEPTH, HBQ), 1.0),
        "norm_attn_pre": gain(ks[13], (DEPTH, D_MODEL)),
        "norm_attn_post": gain(ks[14], (DEPTH, D_MODEL)),
        "w_ff1": nrm(ks[15], (DEPTH, D_MODEL, D_FF), D_MODEL ** -0.5),
        "w_ff2": nrm(ks[16], (DEPTH, D_FF, D_MODEL), D_FF ** -0.5),
        "norm_ff_pre": gain(ks[17], (DEPTH, D_MODEL)),
        "norm_ff_post": gain(ks[18], (DEPTH, D_MODEL)),
    }


def reference(x, meta_tokens, rel_bias, w_in, w_branch_a, w_branch_b, w_out,
              lambda_q1, lambda_k1, lambda_q2, lambda_k2, diff_norm, sinks,
              norm_attn_pre, norm_attn_post, w_ff1, w_ff2, norm_ff_pre, norm_ff_post):
    B = x.shape[0]
    pad = jnp.zeros((B, PAD_FRONT, D_MODEL), x.dtype)
    meta = jnp.broadcast_to(meta_tokens[None].astype(x.dtype), (B, N_META, D_MODEL))
    h = jnp.concatenate([pad, meta, x], axis=1)
    P = h.shape[1]
    rel_a = rel_bias[:, :HA]
    rel_b = rel_bias[:, HA:HA + HBQ]
    for l in range(DEPTH):
        lam_init = 0.8 - 0.6 * math.exp(-0.3 * l)
        lam = (jnp.exp(jnp.sum(lambda_q1[l].astype(jnp.float32) * lambda_k1[l].astype(jnp.float32)))
               - jnp.exp(jnp.sum(lambda_q2[l].astype(jnp.float32) * lambda_k2[l].astype(jnp.float32)))
               + lam_init)
        u = rmsnorm(h, norm_attn_pre[l])
        z = u @ w_in[l]
        qa, ka, va, qb, kb, vb, ga, gb = jnp.split(z, SPLITS, axis=-1)
        ya = diff_attention(qa.reshape(B, P, HA, 2, DHA), ka.reshape(B, P, HA, 2, DHA),
                            va.reshape(B, P, HA, 2 * DHA), rel_a, lam, lam_init, diff_norm[l])
        yb = swa_sink_attention(qb.reshape(B, P, HBQ, DHB), kb.reshape(B, P, HBKV, DHB),
                                vb.reshape(B, P, HBKV, DHB), rel_b, sinks[l])
        mix = jax.nn.sigmoid(ga) * (ya @ w_branch_a[l]) + jax.nn.sigmoid(gb) * (yb @ w_branch_b[l])
        h = h + rmsnorm(mix @ w_out[l], norm_attn_post[l])
        u = rmsnorm(h, norm_ff_pre[l])
        f = jnp.square(jax.nn.relu(u @ w_ff1[l])) @ w_ff2[l]
        h = h + rmsnorm(f, norm_ff_post[l])
    return h[:, BLK:]
```

```python
import functools
import math

import jax
import jax.numpy as jnp
from jax import lax
from jax.experimental import pallas as pl
from jax.experimental.pallas import tpu as pltpu

D_MODEL = 2048
SEQ = 16384
DEPTH = 4
N_META = 16
BLK = 128
WINDOW = 128
HA = 8
DHA = 64
HBQ = 16
HBKV = 2
DHB = 64
G_B = HBQ // HBKV
D_FF = 4 * D_MODEL
N_BUCKETS = 32
MAX_DIST = 128
EPS = 1e-6
NEG = -1e30

QA_W = HA * 2 * DHA
QKV_W = 3 * QA_W + HBQ * DHB + 2 * HBKV * DHB
PAD_FRONT = BLK - N_META
P_STREAM = SEQ + BLK

ROW_TILE = 512
P_PAD = -(-P_STREAM // ROW_TILE) * ROW_TILE
N_TILES = P_PAD // ROW_TILE
N_BLOCKS = P_PAD // BLK
MERGE_TILE = 256
FF_TILE = 1024
MIB = 1024 * 1024

F32 = jnp.float32
BF16 = jnp.bfloat16


def _rms(x, g):
    return x * lax.rsqrt(jnp.mean(x * x, axis=-1, keepdims=True) + EPS) * g


def _rmsnorm_kernel(h_ref, g_ref, o_ref):
    o_ref[...] = _rms(h_ref[...], g_ref[...]).astype(o_ref.dtype)


def _rmsnorm_rows(h, g):
    return pl.pallas_call(
        _rmsnorm_kernel,
        out_shape=jax.ShapeDtypeStruct(h.shape, BF16),
        grid=(N_TILES,),
        in_specs=[pl.BlockSpec((ROW_TILE, D_MODEL), lambda i: (i, 0)),
                  pl.BlockSpec((1, D_MODEL), lambda i: (0, 0))],
        out_specs=pl.BlockSpec((ROW_TILE, D_MODEL), lambda i: (i, 0)),
        compiler_params=pltpu.CompilerParams(dimension_semantics=("parallel",)),
        name="rmsnorm_rows",
    )(h, g.reshape(1, D_MODEL))


def _proj_kernel(u_ref, w_ref, o_ref, *, gate):
    acc = jnp.dot(u_ref[...], w_ref[...], preferred_element_type=F32)
    if gate:
        acc = jax.nn.sigmoid(acc)
    o_ref[...] = acc.astype(o_ref.dtype)


def _proj(u, w, tn, out_dtype, gate, name):
    n = w.shape[1]
    assert n % tn == 0
    return pl.pallas_call(
        functools.partial(_proj_kernel, gate=gate),
        out_shape=jax.ShapeDtypeStruct((P_PAD, n), out_dtype),
        grid=(n // tn, N_TILES),
        in_specs=[pl.BlockSpec((ROW_TILE, D_MODEL), lambda j, i: (i, 0)),
                  pl.BlockSpec((D_MODEL, tn), lambda j, i: (0, j))],
        out_specs=pl.BlockSpec((ROW_TILE, tn), lambda j, i: (i, j)),
        compiler_params=pltpu.CompilerParams(
            dimension_semantics=("parallel", "parallel"), vmem_limit_bytes=48 * MIB),
        name=name,
    )(u, w)


def _merge_kernel(ya_ref, yb_ref, sga_ref, sgb_ref, h_ref, wa_ref, wb_ref, wo_ref,
                  gpost_ref, gnext_ref, h_out_ref, u_out_ref):
    ta = jnp.dot(ya_ref[...], wa_ref[...], preferred_element_type=F32)
    tb = jnp.dot(yb_ref[...], wb_ref[...], preferred_element_type=F32)
    mix = sga_ref[...] * ta + sgb_ref[...] * tb
    t = jnp.dot(mix.astype(BF16), wo_ref[...], preferred_element_type=F32)
    hn = h_ref[...] + _rms(t, gpost_ref[...])
    h_out_ref[...] = hn
    u_out_ref[...] = _rms(hn, gnext_ref[...]).astype(BF16)


def _merge(ya, yb, sg, h, wa, wb, wo, g_post, g_next):
    tm = MERGE_TILE
    once = pl.Buffered(1)
    return pl.pallas_call(
        _merge_kernel,
        out_shape=(jax.ShapeDtypeStruct((P_PAD, D_MODEL), F32),
                   jax.ShapeDtypeStruct((P_PAD, D_MODEL), BF16)),
        grid=(P_PAD // tm,),
        in_specs=[pl.BlockSpec((tm, QA_W), lambda i: (i, 0)),
                  pl.BlockSpec((tm, QA_W), lambda i: (i, 0)),
                  pl.BlockSpec((tm, D_MODEL), lambda i: (i, 0)),
                  pl.BlockSpec((tm, D_MODEL), lambda i: (i, 1)),
                  pl.BlockSpec((tm, D_MODEL), lambda i: (i, 0)),
                  pl.BlockSpec((QA_W, D_MODEL), lambda i: (0, 0), pipeline_mode=once),
                  pl.BlockSpec((QA_W, D_MODEL), lambda i: (0, 0), pipeline_mode=once),
                  pl.BlockSpec((D_MODEL, D_MODEL), lambda i: (0, 0), pipeline_mode=once),
                  pl.BlockSpec((1, D_MODEL), lambda i: (0, 0)),
                  pl.BlockSpec((1, D_MODEL), lambda i: (0, 0))],
        out_specs=(pl.BlockSpec((tm, D_MODEL), lambda i: (i, 0)),
                   pl.BlockSpec((tm, D_MODEL), lambda i: (i, 0))),
        compiler_params=pltpu.CompilerParams(
            dimension_semantics=("parallel",), vmem_limit_bytes=56 * MIB),
        name="gated_merge",
    )(ya, yb, sg, sg, h, wa, wb, wo, g_post.reshape(1, D_MODEL), g_next.reshape(1, D_MODEL))


def _ffn_kernel(u_ref, w1_ref, w2_ref, h_ref, gpost_ref, gnext_ref, h_out_ref, u_out_ref, acc_ref):
    f = pl.program_id(1)

    @pl.when(f == 0)
    def _():
        acc_ref[...] = jnp.zeros_like(acc_ref)

    a = jnp.dot(u_ref[...], w1_ref[...], preferred_element_type=F32)
    a = jnp.square(jnp.maximum(a, 0.0))
    acc_ref[...] += jnp.dot(a.astype(BF16), w2_ref[...], preferred_element_type=F32)

    @pl.when(f == pl.num_programs(1) - 1)
    def _():
        hn = h_ref[...] + _rms(acc_ref[...], gpost_ref[...])
        h_out_ref[...] = hn
        u_out_ref[...] = _rms(hn, gnext_ref[...]).astype(BF16)


def _ffn(u, w1, w2, h, g_post, g_next):
    tm, tf = ROW_TILE, FF_TILE
    return pl.pallas_call(
        _ffn_kernel,
        out_shape=(jax.ShapeDtypeStruct((P_PAD, D_MODEL), F32),
                   jax.ShapeDtypeStruct((P_PAD, D_MODEL), BF16)),
        grid=(N_TILES, D_FF // tf),
        in_specs=[pl.BlockSpec((tm, D_MODEL), lambda i, f: (i, 0)),
                  pl.BlockSpec((D_MODEL, tf), lambda i, f: (0, f)),
                  pl.BlockSpec((tf, D_MODEL), lambda i, f: (f, 0)),
                  pl.BlockSpec((tm, D_MODEL), lambda i, f: (i, 0)),
                  pl.BlockSpec((1, D_MODEL), lambda i, f: (0, 0)),
                  pl.BlockSpec((1, D_MODEL), lambda i, f: (0, 0))],
        out_specs=(pl.BlockSpec((tm, D_MODEL), lambda i, f: (i, 0)),
                   pl.BlockSpec((tm, D_MODEL), lambda i, f: (i, 0))),
        scratch_shapes=[pltpu.VMEM((tm, D_MODEL), F32)],
        compiler_params=pltpu.CompilerParams(
            dimension_semantics=("parallel", "arbitrary"), vmem_limit_bytes=56 * MIB),
        name="ffn_relu2",
    )(u, w1, w2, h, g_post.reshape(1, D_MODEL), g_next.reshape(1, D_MODEL))


def _attn_a_kernel(lamv_ref, subg_ref, qbd_ref, k_ref, vt_ref, bdiag_ref, bcorn_ref, pmask_ref,
                   o_ref, m_ref, l_ref, acc_ref, *, lam_init):
    t = ROW_TILE
    i = pl.program_id(1)
    m_ref[...] = jnp.full(m_ref.shape, NEG, F32)
    l_ref[...] = jnp.zeros_like(l_ref)
    acc_ref[...] = jnp.zeros_like(acc_ref)

    def tile(j, bias):
        ks = pl.multiple_of(j * t, t)
        s = jnp.dot(k_ref[pl.ds(ks, t), :], qbd_ref[...], preferred_element_type=F32)
        if bias is not None:
            s = s + bias
        m_prev = m_ref[...]
        m_new = jnp.maximum(m_prev, jnp.max(s, axis=0, keepdims=True))
        alpha = jnp.exp(m_prev - m_new)
        p = jnp.exp(s - m_new)
        l_ref[...] = alpha * l_ref[...] + jnp.sum(p, axis=0, keepdims=True)
        pv = jnp.dot(vt_ref[j], p.astype(BF16), preferred_element_type=F32)
        acc_ref[...] = alpha * acc_ref[...] + pv
        m_ref[...] = m_new

    def both(b):
        return jnp.concatenate([b, b], axis=1)

    @pl.when(i >= 2)
    def _():
        tile(0, jnp.concatenate([pmask_ref[...]] * (2 * t // BLK), axis=1))

    def far(j, carry):
        tile(j, None)
        return carry

    lax.fori_loop(1, i - 1, far, 0)

    @pl.when(i >= 1)
    def _():
        tile(i - 1, both(bcorn_ref[...]))

    tile(i, both(bdiag_ref[...]))

    lv = lamv_ref[...]
    lam = (jnp.exp(jnp.sum(lv[0:1] * lv[1:2], axis=-1, keepdims=True))
           - jnp.exp(jnp.sum(lv[2:3] * lv[3:4], axis=-1, keepdims=True)) + lam_init)
    o = acc_ref[...] / l_ref[...]
    od = o[:, :t] - lam * o[:, t:]
    y = od * lax.rsqrt(jnp.mean(od * od, axis=0, keepdims=True) + EPS) * subg_ref[...]
    y = y * (1.0 - lam_init)
    o_ref[...] = y.T.astype(o_ref.dtype)


def _attn_a(lamv, subg, qbd, k_all, vt, bdiag, bcorn, pmask, lam_init):
    t = ROW_TILE
    return pl.pallas_call(
        functools.partial(_attn_a_kernel, lam_init=lam_init),
        out_shape=jax.ShapeDtypeStruct((P_PAD, QA_W), BF16),
        grid=(HA, N_TILES),
        in_specs=[pl.BlockSpec((4, DHA), lambda h, i: (0, 0)),
                  pl.BlockSpec((2 * DHA, 1), lambda h, i: (0, 0)),
                  pl.BlockSpec((None, 2 * DHA, 2 * t), lambda h, i: (h, 0, i)),
                  pl.BlockSpec((P_PAD, 2 * DHA), lambda h, i: (0, h)),
                  pl.BlockSpec((None, N_TILES, 2 * DHA, t), lambda h, i: (h, 0, 0, 0)),
                  pl.BlockSpec((None, None, t, t), lambda h, i: (h, jnp.minimum(i, 1), 0, 0)),
                  pl.BlockSpec((None, None, t, t), lambda h, i: (h, jnp.where(i >= 2, 1, 0), 0, 0)),
                  pl.BlockSpec((t, BLK), lambda h, i: (0, 0))],
        out_specs=pl.BlockSpec((t, 2 * DHA), lambda h, i: (i, h)),
        scratch_shapes=[pltpu.VMEM((1, 2 * t), F32), pltpu.VMEM((1, 2 * t), F32),
                        pltpu.VMEM((2 * DHA, 2 * t), F32)],
        compiler_params=pltpu.CompilerParams(
            dimension_semantics=("parallel", "arbitrary"), vmem_limit_bytes=48 * MIB),
        name="diff_attention",
    )(lamv, subg, qbd, k_all, vt, bdiag, bcorn, pmask)


def _attn_b_kernel(q_ref, km_ref, kp_ref, kc_ref, vm_ref, vp_ref, vc_ref, bias_ref, sink_ref, o_ref):
    q = q_ref[...].reshape(G_B * BLK, DHB)
    kcat = jnp.concatenate([km_ref[...], kp_ref[...], kc_ref[...]], axis=0)
    vcat = jnp.concatenate([vm_ref[...], vp_ref[...], vc_ref[...]], axis=0)
    s = lax.dot_general(q, kcat, (((1,), (1,)), ((), ())), preferred_element_type=F32)
    s = s.reshape(G_B, BLK, 3 * BLK) + bias_ref[...]
    sink = sink_ref[...]
    m = jnp.maximum(jnp.max(s, axis=-1, keepdims=True), sink)
    p = jnp.exp(s - m)
    denom = jnp.sum(p, axis=-1, keepdims=True) + jnp.exp(sink - m)
    p = (p / denom).reshape(G_B * BLK, 3 * BLK).astype(BF16)
    o = jnp.dot(p, vcat, preferred_element_type=F32)
    o_ref[...] = o.reshape(G_B, BLK, DHB).astype(o_ref.dtype)


def _attn_b(qb, kb, vb, bias, sinks):
    kv_meta = pl.BlockSpec((None, BLK, DHB), lambda g, n: (g, 0, 0))
    kv_prev = pl.BlockSpec((None, BLK, DHB), lambda g, n: (g, jnp.maximum(n - 1, 0), 0))
    kv_cur = pl.BlockSpec((None, BLK, DHB), lambda g, n: (g, n, 0))
    return pl.pallas_call(
        _attn_b_kernel,
        out_shape=jax.ShapeDtypeStruct((HBQ, P_PAD, DHB), BF16),
        grid=(HBKV, N_BLOCKS),
        in_specs=[pl.BlockSpec((G_B, BLK, DHB), lambda g, n: (g, n, 0)),
                  kv_meta, kv_prev, kv_cur, kv_meta, kv_prev, kv_cur,
                  pl.BlockSpec((G_B, None, BLK, 3 * BLK), lambda g, n: (g, jnp.minimum(n, 2), 0, 0)),
                  pl.BlockSpec((G_B, 1, 1), lambda g, n: (g, 0, 0))],
        out_specs=pl.BlockSpec((G_B, BLK, DHB), lambda g, n: (g, n, 0)),
        compiler_params=pltpu.CompilerParams(dimension_semantics=("parallel", "parallel")),
        name="swa_sink_attention",
    )(qb, kb, kb, kb, vb, vb, vb, bias, sinks.reshape(HBQ, 1, 1))


def _t5_bucket(dist):
    n = jnp.maximum(dist, 0)
    max_exact = N_BUCKETS // 2
    nf = jnp.maximum(n, 1).astype(F32)
    large = max_exact + (jnp.log(nf / max_exact) / math.log(MAX_DIST / max_exact)
                         * (N_BUCKETS - max_exact)).astype(jnp.int32)
    large = jnp.minimum(large, N_BUCKETS - 1)
    return jnp.where(n < max_exact, n, large)


def _bias_tiles_a(rel_a):
    t = ROW_TILE
    table = rel_a[_t5_bucket(jnp.arange(2 * t))].astype(F32) - rel_a[N_BUCKETS - 1].astype(F32)
    kb = jnp.arange(t)[:, None]
    qa = jnp.arange(t)[None, :]
    pad = kb < PAD_FRONT

    def variants(dist, ok):
        v = jnp.where(ok[None], jnp.moveaxis(table[jnp.clip(dist, 0, 2 * t - 1)], -1, 0), NEG)
        return jnp.stack([jnp.where(pad[None], NEG, v), v], axis=1)

    diag = variants(qa - kb, qa >= kb)
    corner = variants(t + qa - kb, jnp.ones((t, t), bool))
    pmask = jnp.broadcast_to(jnp.where(pad, NEG, 0.0).astype(F32), (t, BLK))
    return diag, corner, pmask


def _bias_tiles_b(rel_b):
    table = rel_b[_t5_bucket(jnp.arange(3 * BLK))].astype(F32)
    a = jnp.arange(BLK)[:, None]
    b = jnp.arange(BLK)[None, :]
    is_meta = b >= PAD_FRONT

    def look(dist, ok):
        return jnp.where(ok[None], jnp.moveaxis(table[jnp.clip(dist, 0, 3 * BLK - 1)], -1, 0), NEG)

    none = jnp.zeros((BLK, BLK), bool)
    prev = look(BLK + a - b, b > a)
    cur = look(a - b, a >= b)
    masked = look(a - b, none)
    blocks = []
    for n in range(3):
        dist_m = n * BLK + a - b
        meta = look(dist_m, is_meta & (dist_m >= 0))
        blocks.append(jnp.concatenate([meta, prev if n >= 2 else masked, cur if n >= 1 else masked], axis=-1))
    return jnp.stack(blocks, axis=1)


def kernel(x, meta_tokens, rel_bias, w_in, w_branch_a, w_branch_b, w_out, lambda_q1, lambda_k1,
           lambda_q2, lambda_k2, diff_norm, sinks, norm_attn_pre, norm_attn_post, w_ff1, w_ff2,
           norm_ff_pre, norm_ff_post):
    assert x.shape == (1, SEQ, D_MODEL)
    t = ROW_TILE
    h = jnp.concatenate([jnp.zeros((PAD_FRONT, D_MODEL), F32), meta_tokens.astype(F32), x[0],
                         jnp.zeros((P_PAD - P_STREAM, D_MODEL), F32)], axis=0)
    bdiag, bcorn, pmask = _bias_tiles_a(rel_bias[:, :HA])
    bias_b = _bias_tiles_b(rel_bias[:, HA:HA + HBQ])
    col_scale = jnp.ones((QKV_W,), F32).at[:QA_W].set(DHA ** -0.5).at[3 * QA_W:4 * QA_W].set(DHB ** -0.5)
    sub_rows = (lax.broadcasted_iota(jnp.int32, (1, 2 * DHA, 1, 1), 1) < DHA)

    u = _rmsnorm_rows(h, norm_attn_pre[0])
    for l in range(DEPTH):
        lam_init = 0.8 - 0.6 * math.exp(-0.3 * l)
        w_qkv = (w_in[l][:, :QKV_W] * col_scale).astype(BF16)
        w_gate = w_in[l][:, QKV_W:].astype(BF16)
        z = _proj(u, w_qkv, QKV_W // 2, BF16, False, "proj_qkv")
        sg = _proj(u, w_gate, 1024, F32, True, "proj_gates")

        q_t = z[:, :QA_W].reshape(N_TILES, t, HA, 2 * DHA).transpose(2, 3, 0, 1)
        qbd = jnp.stack([jnp.where(sub_rows, q_t, 0), jnp.where(sub_rows, 0, q_t)],
                        axis=3).reshape(HA, 2 * DHA, N_TILES * 2 * t)
        k_all = z[:, QA_W:2 * QA_W]
        vt = z[:, 2 * QA_W:3 * QA_W].reshape(N_TILES, t, HA, 2 * DHA).transpose(2, 0, 3, 1)
        lamv = jnp.stack([lambda_q1[l], lambda_k1[l], lambda_q2[l], lambda_k2[l]]).astype(F32)
        ya = _attn_a(lamv, diff_norm[l].astype(F32).reshape(2 * DHA, 1), qbd, k_all, vt,
                     bdiag, bcorn, pmask, lam_init)

        qb = z[:, 3 * QA_W:4 * QA_W].reshape(P_PAD, HBQ, DHB).transpose(1, 0, 2)
        kb = z[:, 4 * QA_W:4 * QA_W + HBKV * DHB].reshape(P_PAD, HBKV, DHB).transpose(1, 0, 2)
        vb = z[:, 4 * QA_W + HBKV * DHB:].reshape(P_PAD, HBKV, DHB).transpose(1, 0, 2)
        yb = _attn_b(qb, kb, vb, bias_b, sinks[l].astype(F32))
        yb = yb.transpose(1, 0, 2).reshape(P_PAD, HBQ * DHB)

        h, u = _merge(ya, yb, sg, h, w_branch_a[l].astype(BF16), w_branch_b[l].astype(BF16),
                      w_out[l].astype(BF16), norm_attn_post[l], norm_ff_pre[l])
        h, u = _ffn(u, w_ff1[l].astype(BF16), w_ff2[l].astype(BF16), h, norm_ff_post[l],
                    norm_attn_pre[(l + 1) % DEPTH])
    return h[BLK:BLK + SEQ][None]
```

```python
import functools
import math

import jax
import jax.numpy as jnp
from jax import lax
from jax.experimental import pallas as pl
from jax.experimental.pallas import tpu as pltpu

D_MODEL = 2048
SEQ = 16384
DEPTH = 4
N_META = 16
BLK = 128
WINDOW = 128
HA = 8
DHA = 64
HBQ = 16
HBKV = 2
DHB = 64
G_B = HBQ // HBKV
D_FF = 4 * D_MODEL
N_BUCKETS = 32
MAX_DIST = 128
EPS = 1e-6
NEG = -1e30
LOG2E = math.log2(math.e)

QA_W = HA * 2 * DHA
QKV_W = 3 * QA_W + HBQ * DHB + 2 * HBKV * DHB
PAD_FRONT = BLK - N_META
P_STREAM = SEQ + BLK

ROW_TILE = 512
P_PAD = -(-P_STREAM // ROW_TILE) * ROW_TILE
N_TILES = P_PAD // ROW_TILE
N_BLOCKS = P_PAD // BLK
BF16_SUBLANES = 16
VT_ROWS = 2 * DHA + BF16_SUBLANES
MERGE_TILE = 256
FF_TILE = 1024
MIB = 1024 * 1024

F32 = jnp.float32
BF16 = jnp.bfloat16


def _rms(x, g):
    return x * lax.rsqrt(jnp.mean(x * x, axis=-1, keepdims=True) + EPS) * g


def _rmsnorm_kernel(h_ref, g_ref, o_ref):
    o_ref[...] = _rms(h_ref[...], g_ref[...]).astype(o_ref.dtype)


def _rmsnorm_rows(h, g):
    return pl.pallas_call(
        _rmsnorm_kernel,
        out_shape=jax.ShapeDtypeStruct(h.shape, BF16),
        grid=(N_TILES,),
        in_specs=[pl.BlockSpec((ROW_TILE, D_MODEL), lambda i: (i, 0)),
                  pl.BlockSpec((1, D_MODEL), lambda i: (0, 0))],
        out_specs=pl.BlockSpec((ROW_TILE, D_MODEL), lambda i: (i, 0)),
        compiler_params=pltpu.CompilerParams(dimension_semantics=("parallel",)),
        name="rmsnorm_rows",
    )(h, g.reshape(1, D_MODEL))


def _proj_kernel(u_ref, w_ref, o_ref, *, gate):
    acc = jnp.dot(u_ref[...], w_ref[...], preferred_element_type=F32)
    if gate:
        acc = jax.nn.sigmoid(acc)
    o_ref[...] = acc.astype(o_ref.dtype)


def _proj(u, w, tn, out_dtype, gate, name):
    n = w.shape[1]
    assert n % tn == 0
    return pl.pallas_call(
        functools.partial(_proj_kernel, gate=gate),
        out_shape=jax.ShapeDtypeStruct((P_PAD, n), out_dtype),
        grid=(n // tn, N_TILES),
        in_specs=[pl.BlockSpec((ROW_TILE, D_MODEL), lambda j, i: (i, 0)),
                  pl.BlockSpec((D_MODEL, tn), lambda j, i: (0, j))],
        out_specs=pl.BlockSpec((ROW_TILE, tn), lambda j, i: (i, j)),
        compiler_params=pltpu.CompilerParams(
            dimension_semantics=("parallel", "parallel"), vmem_limit_bytes=48 * MIB),
        name=name,
    )(u, w)


def _merge_kernel(ya_ref, yb_ref, sga_ref, sgb_ref, h_ref, wa_ref, wb_ref, wo_ref,
                  gpost_ref, gnext_ref, h_out_ref, u_out_ref):
    ta = jnp.dot(ya_ref[...], wa_ref[...], preferred_element_type=F32)
    tb = jnp.dot(yb_ref[...], wb_ref[...], preferred_element_type=F32)
    mix = sga_ref[...] * ta + sgb_ref[...] * tb
    t = jnp.dot(mix.astype(BF16), wo_ref[...], preferred_element_type=F32)
    hn = h_ref[...] + _rms(t, gpost_ref[...])
    h_out_ref[...] = hn
    u_out_ref[...] = _rms(hn, gnext_ref[...]).astype(BF16)


def _merge(ya, yb, sg, h, wa, wb, wo, g_post, g_next):
    tm = MERGE_TILE
    once = pl.Buffered(1)
    return pl.pallas_call(
        _merge_kernel,
        out_shape=(jax.ShapeDtypeStruct((P_PAD, D_MODEL), F32),
                   jax.ShapeDtypeStruct((P_PAD, D_MODEL), BF16)),
        grid=(P_PAD // tm,),
        in_specs=[pl.BlockSpec((tm, QA_W), lambda i: (i, 0)),
                  pl.BlockSpec((tm, QA_W), lambda i: (i, 0)),
                  pl.BlockSpec((tm, D_MODEL), lambda i: (i, 0)),
                  pl.BlockSpec((tm, D_MODEL), lambda i: (i, 1)),
                  pl.BlockSpec((tm, D_MODEL), lambda i: (i, 0)),
                  pl.BlockSpec((QA_W, D_MODEL), lambda i: (0, 0), pipeline_mode=once),
                  pl.BlockSpec((QA_W, D_MODEL), lambda i: (0, 0), pipeline_mode=once),
                  pl.BlockSpec((D_MODEL, D_MODEL), lambda i: (0, 0), pipeline_mode=once),
                  pl.BlockSpec((1, D_MODEL), lambda i: (0, 0)),
                  pl.BlockSpec((1, D_MODEL), lambda i: (0, 0))],
        out_specs=(pl.BlockSpec((tm, D_MODEL), lambda i: (i, 0)),
                   pl.BlockSpec((tm, D_MODEL), lambda i: (i, 0))),
        compiler_params=pltpu.CompilerParams(
            dimension_semantics=("parallel",), vmem_limit_bytes=56 * MIB),
        name="gated_merge",
    )(ya, yb, sg, sg, h, wa, wb, wo, g_post.reshape(1, D_MODEL), g_next.reshape(1, D_MODEL))


def _ffn_kernel(u_ref, w1_ref, w2_ref, h_ref, gpost_ref, gnext_ref, h_out_ref, u_out_ref, acc_ref):
    f = pl.program_id(1)

    @pl.when(f == 0)
    def _():
        acc_ref[...] = jnp.zeros_like(acc_ref)

    a = jnp.dot(u_ref[...], w1_ref[...], preferred_element_type=F32)
    a = jnp.square(jnp.maximum(a, 0.0))
    acc_ref[...] += jnp.dot(a.astype(BF16), w2_ref[...], preferred_element_type=F32)

    @pl.when(f == pl.num_programs(1) - 1)
    def _():
        hn = h_ref[...] + _rms(acc_ref[...], gpost_ref[...])
        h_out_ref[...] = hn
        u_out_ref[...] = _rms(hn, gnext_ref[...]).astype(BF16)


def _ffn(u, w1, w2, h, g_post, g_next):
    tm, tf = ROW_TILE, FF_TILE
    return pl.pallas_call(
        _ffn_kernel,
        out_shape=(jax.ShapeDtypeStruct((P_PAD, D_MODEL), F32),
                   jax.ShapeDtypeStruct((P_PAD, D_MODEL), BF16)),
        grid=(N_TILES, D_FF // tf),
        in_specs=[pl.BlockSpec((tm, D_MODEL), lambda i, f: (i, 0)),
                  pl.BlockSpec((D_MODEL, tf), lambda i, f: (0, f)),
                  pl.BlockSpec((tf, D_MODEL), lambda i, f: (f, 0)),
                  pl.BlockSpec((tm, D_MODEL), lambda i, f: (i, 0)),
                  pl.BlockSpec((1, D_MODEL), lambda i, f: (0, 0)),
                  pl.BlockSpec((1, D_MODEL), lambda i, f: (0, 0))],
        out_specs=(pl.BlockSpec((tm, D_MODEL), lambda i, f: (i, 0)),
                   pl.BlockSpec((tm, D_MODEL), lambda i, f: (i, 0))),
        scratch_shapes=[pltpu.VMEM((tm, D_MODEL), F32)],
        compiler_params=pltpu.CompilerParams(
            dimension_semantics=("parallel", "arbitrary"), vmem_limit_bytes=56 * MIB),
        name="ffn_relu2",
    )(u, w1, w2, h, g_post.reshape(1, D_MODEL), g_next.reshape(1, D_MODEL))


def _attn_a_kernel(lamv_ref, subg_ref, qbd_ref, k_ref, vt_ref, bdiag_ref, bcorn_ref, bcol_c_ref, bcol_d_ref,
                   o_ref, s0_ref, s1_ref, t0_ref, t1_ref, p0_ref, p1_ref, a0_ref, a1_ref, m_ref, acc_ref,
                   *, lam_init):
    t = ROW_TILE
    i = pl.program_id(1)
    n_plain = jnp.maximum(i - 2, 0)
    odd = n_plain % 2
    m_ref[...] = jnp.full(m_ref.shape, NEG, F32)
    acc_ref[...] = jnp.zeros_like(acc_ref)

    def scores(j, s_ref, t_ref, bias):
        ks = pl.multiple_of(j * t, t)
        s = jnp.dot(k_ref[pl.ds(ks, t), :], qbd_ref[...], preferred_element_type=F32)
        if bias is not None:
            s = s + bias
        s_ref[...] = s
        t_ref[...] = jnp.max(s, axis=0, keepdims=True)

    def softmax(s_ref, t_ref, p_ref, a_ref):
        m_prev = m_ref[...]
        m_new = jnp.maximum(m_prev, t_ref[...])
        p_ref[...] = jnp.exp2(s_ref[...] - m_new).astype(BF16)
        a_ref[...] = jnp.exp2(m_prev - m_new)
        m_ref[...] = m_new

    def values(j, p_ref, a_ref):
        pv = jnp.dot(vt_ref[j], p_ref[...], preferred_element_type=F32)
        acc_ref[...] = a_ref[...] * acc_ref[...] + pv

    def square(b):
        return jnp.concatenate([b, b], axis=1)

    def column(b):
        return jnp.concatenate([b] * (2 * t // BLK), axis=1)

    even = (s0_ref, t0_ref, p0_ref, a0_ref)
    odd_set = (s1_ref, t1_ref, p1_ref, a1_ref)

    def step(j_scores, bias, score_set, j_values):
        other = odd_set if score_set is even else even
        if j_values is not None:
            values(j_values, score_set[2], score_set[3])
        scores(j_scores, score_set[0], score_set[1], bias)
        softmax(*other)

    j_b = jnp.maximum(i - 1, 0)
    scores(i, s0_ref, t0_ref, square(bdiag_ref[...]))
    step(j_b, square(bcorn_ref[...]), odd_set, None)
    step(0, column(bcol_c_ref[...]), even, i)
    step(1, column(bcol_d_ref[...]), odd_set, j_b)

    def pair(q, carry):
        j_even, j_odd = carry
        j0 = 1 + odd + 2 * q
        step(j0, None, even, j_even)
        step(j0 + 1, None, odd_set, j_odd)
        return j0, j0 + 1

    j_even, j_odd = lax.fori_loop(0, n_plain // 2, pair, (jnp.int32(0), jnp.int32(1)))
    softmax(*odd_set)
    values(j_even, p0_ref, a0_ref)
    values(j_odd, p1_ref, a1_ref)

    lv = lamv_ref[...]
    lam = (jnp.exp(jnp.sum(lv[0:1] * lv[1:2], axis=-1, keepdims=True))
           - jnp.exp(jnp.sum(lv[2:3] * lv[3:4], axis=-1, keepdims=True)) + lam_init)
    acc = acc_ref[...]
    o = acc[:2 * DHA] / acc[2 * DHA:2 * DHA + 1]
    od = o[:, :t] - lam * o[:, t:]
    y = od * lax.rsqrt(jnp.mean(od * od, axis=0, keepdims=True) + EPS) * subg_ref[...]
    y = y * (1.0 - lam_init)
    o_ref[...] = y.T.astype(o_ref.dtype)


def _attn_a(lamv, subg, qbd, k_all, vt, bdiag, bcorn, bcol, lam_init):
    t = ROW_TILE

    def odd_plain(i):
        return jnp.maximum(i - 2, 0) % 2

    return pl.pallas_call(
        functools.partial(_attn_a_kernel, lam_init=lam_init),
        out_shape=jax.ShapeDtypeStruct((P_PAD, QA_W), BF16),
        grid=(HA, N_TILES),
        in_specs=[pl.BlockSpec((4, DHA), lambda h, i: (0, 0)),
                  pl.BlockSpec((2 * DHA, 1), lambda h, i: (0, 0)),
                  pl.BlockSpec((None, 2 * DHA, 2 * t), lambda h, i: (h, 0, i)),
                  pl.BlockSpec((P_PAD, 2 * DHA), lambda h, i: (0, h)),
                  pl.BlockSpec((None, N_TILES, VT_ROWS, t), lambda h, i: (h, 0, 0, 0)),
                  pl.BlockSpec((None, None, t, t), lambda h, i: (h, jnp.minimum(i, 1), 0, 0)),
                  pl.BlockSpec((None, None, t, t), lambda h, i: (h, jnp.minimum(i, 2), 0, 0)),
                  pl.BlockSpec((None, t, BLK), lambda h, i: (jnp.where(i >= 2, 1, 0), 0, 0)),
                  pl.BlockSpec((None, t, BLK), lambda h, i: (2 * odd_plain(i), 0, 0))],
        out_specs=pl.BlockSpec((t, 2 * DHA), lambda h, i: (i, h)),
        scratch_shapes=[pltpu.VMEM((t, 2 * t), F32), pltpu.VMEM((t, 2 * t), F32),
                        pltpu.VMEM((1, 2 * t), F32), pltpu.VMEM((1, 2 * t), F32),
                        pltpu.VMEM((t, 2 * t), BF16), pltpu.VMEM((t, 2 * t), BF16),
                        pltpu.VMEM((1, 2 * t), F32), pltpu.VMEM((1, 2 * t), F32),
                        pltpu.VMEM((1, 2 * t), F32),
                        pltpu.VMEM((VT_ROWS, 2 * t), F32)],
        compiler_params=pltpu.CompilerParams(
            dimension_semantics=("parallel", "arbitrary"), vmem_limit_bytes=48 * MIB),
        name="diff_attention",
    )(lamv, subg, qbd, k_all, vt, bdiag, bcorn, bcol, bcol)


def _attn_b_kernel(q_ref, km_ref, kp_ref, kc_ref, vm_ref, vp_ref, vc_ref, bias_ref, sink_ref, o_ref):
    q = q_ref[...].reshape(G_B * BLK, DHB)
    kcat = jnp.concatenate([km_ref[...], kp_ref[...], kc_ref[...]], axis=0)
    vcat = jnp.concatenate([vm_ref[...], vp_ref[...], vc_ref[...]], axis=0)
    s = lax.dot_general(q, kcat, (((1,), (1,)), ((), ())), preferred_element_type=F32)
    s = s.reshape(G_B, BLK, 3 * BLK) + bias_ref[...]
    sink = sink_ref[...]
    m = jnp.maximum(jnp.max(s, axis=-1, keepdims=True), sink)
    p = jnp.exp(s - m)
    denom = jnp.sum(p, axis=-1, keepdims=True) + jnp.exp(sink - m)
    p = (p / denom).reshape(G_B * BLK, 3 * BLK).astype(BF16)
    o = jnp.dot(p, vcat, preferred_element_type=F32)
    o_ref[...] = o.reshape(G_B, BLK, DHB).astype(o_ref.dtype)


def _attn_b(qb, kb, vb, bias, sinks):
    kv_meta = pl.BlockSpec((None, BLK, DHB), lambda g, n: (g, 0, 0))
    kv_prev = pl.BlockSpec((None, BLK, DHB), lambda g, n: (g, jnp.maximum(n - 1, 0), 0))
    kv_cur = pl.BlockSpec((None, BLK, DHB), lambda g, n: (g, n, 0))
    return pl.pallas_call(
        _attn_b_kernel,
        out_shape=jax.ShapeDtypeStruct((HBQ, P_PAD, DHB), BF16),
        grid=(HBKV, N_BLOCKS),
        in_specs=[pl.BlockSpec((G_B, BLK, DHB), lambda g, n: (g, n, 0)),
                  kv_meta, kv_prev, kv_cur, kv_meta, kv_prev, kv_cur,
                  pl.BlockSpec((G_B, None, BLK, 3 * BLK), lambda g, n: (g, jnp.minimum(n, 2), 0, 0)),
                  pl.BlockSpec((G_B, 1, 1), lambda g, n: (g, 0, 0))],
        out_specs=pl.BlockSpec((G_B, BLK, DHB), lambda g, n: (g, n, 0)),
        compiler_params=pltpu.CompilerParams(dimension_semantics=("parallel", "parallel")),
        name="swa_sink_attention",
    )(qb, kb, kb, kb, vb, vb, vb, bias, sinks.reshape(HBQ, 1, 1))


def _t5_bucket(dist):
    n = jnp.maximum(dist, 0)
    max_exact = N_BUCKETS // 2
    nf = jnp.maximum(n, 1).astype(F32)
    large = max_exact + (jnp.log(nf / max_exact) / math.log(MAX_DIST / max_exact)
                         * (N_BUCKETS - max_exact)).astype(jnp.int32)
    large = jnp.minimum(large, N_BUCKETS - 1)
    return jnp.where(n < max_exact, n, large)


def _bias_tiles_a(rel_a):
    t = ROW_TILE
    span = 2 * t
    table = rel_a[_t5_bucket(jnp.arange(span))].astype(F32) - rel_a[N_BUCKETS - 1].astype(F32)
    table = table.T * LOG2E

    def toeplitz(v):
        flat = jnp.broadcast_to(v[:, None, :], (HA, t, span)).reshape(HA, t * span)
        return flat[:, :t * (span - 1)].reshape(HA, t, span - 1)[:, :, :t]

    kb = jnp.arange(t)[:, None]
    qa = jnp.arange(t)[None, :]
    pad = (kb < PAD_FRONT)[None]
    causal = jnp.where((qa >= kb)[None], toeplitz(table), NEG)
    corner = toeplitz(jnp.roll(table, -t, axis=1))
    diag = jnp.stack([jnp.where(pad, NEG, causal), causal], axis=1)
    corner = jnp.stack([jnp.full_like(corner, NEG), jnp.where(pad, NEG, corner), corner], axis=1)
    pad_col = jnp.broadcast_to(jnp.where(pad[0], NEG, 0.0).astype(F32), (t, BLK))
    column = jnp.stack([jnp.full_like(pad_col, NEG), pad_col, jnp.zeros_like(pad_col)])
    return diag, corner, column


def _bias_tiles_b(rel_b):
    table = rel_b[_t5_bucket(jnp.arange(3 * BLK))].astype(F32)
    a = jnp.arange(BLK)[:, None]
    b = jnp.arange(BLK)[None, :]
    is_meta = b >= PAD_FRONT

    def look(dist, ok):
        return jnp.where(ok[None], jnp.moveaxis(table[jnp.clip(dist, 0, 3 * BLK - 1)], -1, 0), NEG)

    none = jnp.zeros((BLK, BLK), bool)
    prev = look(BLK + a - b, b > a)
    cur = look(a - b, a >= b)
    masked = look(a - b, none)
    blocks = []
    for n in range(3):
        dist_m = n * BLK + a - b
        meta = look(dist_m, is_meta & (dist_m >= 0))
        blocks.append(jnp.concatenate([meta, prev if n >= 2 else masked, cur if n >= 1 else masked], axis=-1))
    return jnp.stack(blocks, axis=1)


def kernel(x, meta_tokens, rel_bias, w_in, w_branch_a, w_branch_b, w_out, lambda_q1, lambda_k1,
           lambda_q2, lambda_k2, diff_norm, sinks, norm_attn_pre, norm_attn_post, w_ff1, w_ff2,
           norm_ff_pre, norm_ff_post):
    assert x.shape == (1, SEQ, D_MODEL)
    t = ROW_TILE
    h = jnp.concatenate([jnp.zeros((PAD_FRONT, D_MODEL), F32), meta_tokens.astype(F32), x[0],
                         jnp.zeros((P_PAD - P_STREAM, D_MODEL), F32)], axis=0)
    bdiag, bcorn, bcol = _bias_tiles_a(rel_bias[:, :HA])
    bias_b = _bias_tiles_b(rel_bias[:, HA:HA + HBQ])
    col_scale = (jnp.ones((QKV_W,), F32).at[:QA_W].set(DHA ** -0.5 * LOG2E)
                 .at[3 * QA_W:4 * QA_W].set(DHB ** -0.5))
    sub_rows = (lax.broadcasted_iota(jnp.int32, (1, 2 * DHA, 1, 1), 1) < DHA)

    u = _rmsnorm_rows(h, norm_attn_pre[0])
    for l in range(DEPTH):
        lam_init = 0.8 - 0.6 * math.exp(-0.3 * l)
        w_qkv = (w_in[l][:, :QKV_W] * col_scale).astype(BF16)
        w_gate = w_in[l][:, QKV_W:].astype(BF16)
        z = _proj(u, w_qkv, QKV_W // 2, BF16, False, "proj_qkv")
        sg = _proj(u, w_gate, 1024, F32, True, "proj_gates")

        q_t = z[:, :QA_W].reshape(N_TILES, t, HA, 2 * DHA).transpose(2, 3, 0, 1)
        qbd = jnp.stack([jnp.where(sub_rows, q_t, 0), jnp.where(sub_rows, 0, q_t)],
                        axis=3).reshape(HA, 2 * DHA, N_TILES * 2 * t)
        k_all = z[:, QA_W:2 * QA_W]
        vt = z[:, 2 * QA_W:3 * QA_W].reshape(N_TILES, t, HA, 2 * DHA).transpose(2, 0, 3, 1)
        vt = jnp.concatenate([vt, jnp.ones((HA, N_TILES, BF16_SUBLANES, t), BF16)], axis=2)
        lamv = jnp.stack([lambda_q1[l], lambda_k1[l], lambda_q2[l], lambda_k2[l]]).astype(F32)
        ya = _attn_a(lamv, diff_norm[l].astype(F32).reshape(2 * DHA, 1), qbd, k_all, vt,
                     bdiag, bcorn, bcol, lam_init)

        qb = z[:, 3 * QA_W:4 * QA_W].reshape(P_PAD, HBQ, DHB).transpose(1, 0, 2)
        kb = z[:, 4 * QA_W:4 * QA_W + HBKV * DHB].reshape(P_PAD, HBKV, DHB).transpose(1, 0, 2)
        vb = z[:, 4 * QA_W + HBKV * DHB:].reshape(P_PAD, HBKV, DHB).transpose(1, 0, 2)
        yb = _attn_b(qb, kb, vb, bias_b, sinks[l].astype(F32))
        yb = yb.transpose(1, 0, 2).reshape(P_PAD, HBQ * DHB)

        h, u = _merge(ya, yb, sg, h, w_branch_a[l].astype(BF16), w_branch_b[l].astype(BF16),
                      w_out[l].astype(BF16), norm_attn_post[l], norm_ff_pre[l])
        h, u = _ffn(u, w_ff1[l].astype(BF16), w_ff2[l].astype(BF16), h, norm_ff_post[l],
                    norm_attn_pre[(l + 1) % DEPTH])
    return h[BLK:BLK + SEQ][None]
```

```python
import functools
import math

import jax
import jax.numpy as jnp
from jax import lax
from jax.experimental import pallas as pl
from jax.experimental.pallas import tpu as pltpu

D_MODEL = 2048
SEQ = 16384
DEPTH = 4
N_META = 16
BLK = 128
WINDOW = 128
HA = 8
DHA = 64
HBQ = 16
HBKV = 2
DHB = 64
G_B = HBQ // HBKV
D_FF = 4 * D_MODEL
N_BUCKETS = 32
MAX_DIST = 128
EPS = 1e-6
NEG = -1e30
LOG2E = math.log2(math.e)

QA_W = HA * 2 * DHA
QB_W = HBQ * DHB
GATE_OFF = 3 * QA_W + QB_W + 2 * HBKV * DHB
PAD_FRONT = BLK - N_META
P_STREAM = SEQ + BLK

LANES = 128
BF16_SUBLANES = 16
ROW_TILE = 512
P_PAD = -(-P_STREAM // ROW_TILE) * ROW_TILE
N_TILES = P_PAD // ROW_TILE
N_BLOCKS = P_PAD // BLK
VT_ROWS = 2 * DHA + BF16_SUBLANES
QX_ROWS = 2 * LANES
PAIR_ROWS = (G_B // 2) * BLK
BAND = 3 * BLK
NAT_W = QA_W + QB_W + 4 * HBKV * DHB
MERGE_TILE = 256
FF_TILE = 1024
MIB = 1024 * 1024

F32 = jnp.float32
BF16 = jnp.bfloat16
NT_DIMS = (((1,), (1,)), ((), ()))


def _rms(x, g):
    return x * lax.rsqrt(jnp.mean(x * x, axis=-1, keepdims=True) + EPS) * g


def _rmsnorm_kernel(h_ref, g_ref, o_ref):
    o_ref[...] = _rms(h_ref[...], g_ref[...]).astype(o_ref.dtype)


def _rmsnorm_rows(h, g):
    return pl.pallas_call(
        _rmsnorm_kernel,
        out_shape=jax.ShapeDtypeStruct(h.shape, BF16),
        grid=(N_TILES,),
        in_specs=[pl.BlockSpec((ROW_TILE, D_MODEL), lambda i: (i, 0)),
                  pl.BlockSpec((1, D_MODEL), lambda i: (0, 0))],
        out_specs=pl.BlockSpec((ROW_TILE, D_MODEL), lambda i: (i, 0)),
        compiler_params=pltpu.CompilerParams(dimension_semantics=("parallel",)),
        name="rmsnorm_rows",
    )(h, g.reshape(1, D_MODEL))


def _proj_kernel(u_ref, wqt_ref, wvt_ref, wn_ref, qbd_ref, vt_ref, zn_ref):
    t = ROW_TILE
    u = u_ref[...]
    qt = lax.dot_general(wqt_ref[...], u, NT_DIMS, preferred_element_type=F32).reshape(HA, 2 * DHA, t)
    top = lax.broadcasted_iota(jnp.int32, (1, 2 * DHA, 1), 1) < DHA
    zero = jnp.zeros_like(qt)
    qbd_ref[...] = jnp.concatenate([jnp.where(top, qt, zero), jnp.where(top, zero, qt)], axis=2).astype(BF16)
    vt = lax.dot_general(wvt_ref[...], u, NT_DIMS, preferred_element_type=F32).reshape(HA, 2 * DHA, t)
    vt_ref[:, :2 * DHA, :] = vt.astype(BF16)
    vt_ref[:, 2 * DHA:, :] = jnp.ones((HA, BF16_SUBLANES, t), BF16)
    zn_ref[...] = jnp.dot(u, wn_ref[...], preferred_element_type=F32).astype(BF16)


def _proj(u, wqt, wvt, wn):
    t = ROW_TILE
    once = pl.Buffered(1)
    return pl.pallas_call(
        _proj_kernel,
        out_shape=(jax.ShapeDtypeStruct((HA, 2 * DHA, N_TILES * 2 * t), BF16),
                   jax.ShapeDtypeStruct((HA, N_TILES, VT_ROWS, t), BF16),
                   jax.ShapeDtypeStruct((P_PAD, NAT_W), BF16)),
        grid=(N_TILES,),
        in_specs=[pl.BlockSpec((t, D_MODEL), lambda i: (i, 0)),
                  pl.BlockSpec((QA_W, D_MODEL), lambda i: (0, 0), pipeline_mode=once),
                  pl.BlockSpec((QA_W, D_MODEL), lambda i: (0, 0), pipeline_mode=once),
                  pl.BlockSpec((D_MODEL, NAT_W), lambda i: (0, 0), pipeline_mode=once)],
        out_specs=(pl.BlockSpec((HA, 2 * DHA, 2 * t), lambda i: (0, 0, i)),
                   pl.BlockSpec((HA, None, VT_ROWS, t), lambda i: (0, i, 0, 0)),
                   pl.BlockSpec((t, NAT_W), lambda i: (i, 0))),
        compiler_params=pltpu.CompilerParams(
            dimension_semantics=("parallel",), vmem_limit_bytes=56 * MIB),
        name="proj_mixers",
    )(u, wqt, wvt, wn)


def _merge_kernel(u_ref, ya_ref, yb_ref, h_ref, wg_ref, wa_ref, wb_ref, wo_ref,
                  gpost_ref, gnext_ref, h_out_ref, u_out_ref):
    u = u_ref[...]
    gate_a = jax.nn.sigmoid(jnp.dot(u, wg_ref[:, :D_MODEL], preferred_element_type=F32))
    mix = gate_a * jnp.dot(ya_ref[...], wa_ref[...], preferred_element_type=F32)
    gate_b = jax.nn.sigmoid(jnp.dot(u, wg_ref[:, D_MODEL:], preferred_element_type=F32))
    mix = mix + gate_b * jnp.dot(yb_ref[...], wb_ref[...], preferred_element_type=F32)
    t = jnp.dot(mix.astype(BF16), wo_ref[...], preferred_element_type=F32)
    hn = h_ref[...] + _rms(t, gpost_ref[...])
    h_out_ref[...] = hn
    u_out_ref[...] = _rms(hn, gnext_ref[...]).astype(BF16)


def _merge(u, ya, yb, h, wg, wa, wb, wo, g_post, g_next):
    tm = MERGE_TILE
    once = pl.Buffered(1)
    return pl.pallas_call(
        _merge_kernel,
        out_shape=(jax.ShapeDtypeStruct((P_PAD, D_MODEL), F32),
                   jax.ShapeDtypeStruct((P_PAD, D_MODEL), BF16)),
        grid=(P_PAD // tm,),
        in_specs=[pl.BlockSpec((tm, D_MODEL), lambda i: (i, 0)),
                  pl.BlockSpec((tm, QA_W), lambda i: (i, 0)),
                  pl.BlockSpec((tm, QB_W), lambda i: (i, 0)),
                  pl.BlockSpec((tm, D_MODEL), lambda i: (i, 0)),
                  pl.BlockSpec((D_MODEL, 2 * D_MODEL), lambda i: (0, 0), pipeline_mode=once),
                  pl.BlockSpec((QA_W, D_MODEL), lambda i: (0, 0), pipeline_mode=once),
                  pl.BlockSpec((QB_W, D_MODEL), lambda i: (0, 0), pipeline_mode=once),
                  pl.BlockSpec((D_MODEL, D_MODEL), lambda i: (0, 0), pipeline_mode=once),
                  pl.BlockSpec((1, D_MODEL), lambda i: (0, 0)),
                  pl.BlockSpec((1, D_MODEL), lambda i: (0, 0))],
        out_specs=(pl.BlockSpec((tm, D_MODEL), lambda i: (i, 0)),
                   pl.BlockSpec((tm, D_MODEL), lambda i: (i, 0))),
        compiler_params=pltpu.CompilerParams(
            dimension_semantics=("parallel",), vmem_limit_bytes=60 * MIB),
        name="gated_merge",
    )(u, ya, yb, h, wg, wa, wb, wo, g_post.reshape(1, D_MODEL), g_next.reshape(1, D_MODEL))


def _ffn_kernel(u_ref, w1_ref, w2_ref, h_ref, gpost_ref, gnext_ref, h_out_ref, u_out_ref, acc_ref):
    f = pl.program_id(1)

    @pl.when(f == 0)
    def _():
        acc_ref[...] = jnp.zeros_like(acc_ref)

    a = jnp.dot(u_ref[...], w1_ref[...], preferred_element_type=F32)
    a = jnp.square(jnp.maximum(a, 0.0))
    acc_ref[...] += jnp.dot(a.astype(BF16), w2_ref[...], preferred_element_type=F32)

    @pl.when(f == pl.num_programs(1) - 1)
    def _():
        hn = h_ref[...] + _rms(acc_ref[...], gpost_ref[...])
        h_out_ref[...] = hn
        u_out_ref[...] = _rms(hn, gnext_ref[...]).astype(BF16)


def _ffn(u, w1, w2, h, g_post, g_next):
    tm, tf = ROW_TILE, FF_TILE
    return pl.pallas_call(
        _ffn_kernel,
        out_shape=(jax.ShapeDtypeStruct((P_PAD, D_MODEL), F32),
                   jax.ShapeDtypeStruct((P_PAD, D_MODEL), BF16)),
        grid=(N_TILES, D_FF // tf),
        in_specs=[pl.BlockSpec((tm, D_MODEL), lambda i, f: (i, 0)),
                  pl.BlockSpec((D_MODEL, tf), lambda i, f: (0, f)),
                  pl.BlockSpec((tf, D_MODEL), lambda i, f: (f, 0)),
                  pl.BlockSpec((tm, D_MODEL), lambda i, f: (i, 0)),
                  pl.BlockSpec((1, D_MODEL), lambda i, f: (0, 0)),
                  pl.BlockSpec((1, D_MODEL), lambda i, f: (0, 0))],
        out_specs=(pl.BlockSpec((tm, D_MODEL), lambda i, f: (i, 0)),
                   pl.BlockSpec((tm, D_MODEL), lambda i, f: (i, 0))),
        scratch_shapes=[pltpu.VMEM((tm, D_MODEL), F32)],
        compiler_params=pltpu.CompilerParams(
            dimension_semantics=("parallel", "arbitrary"), vmem_limit_bytes=56 * MIB),
        name="ffn_relu2",
    )(u, w1, w2, h, g_post.reshape(1, D_MODEL), g_next.reshape(1, D_MODEL))


def _attn_a_kernel(lamv_ref, subg_ref, qbd_ref, k_ref, kx_ref, vt_ref, bdiag_ref, bcorn_ref, bpad_ref,
                   o_ref, qx_ref, m_ref, acc_ref, *, lam_init):
    t = ROW_TILE
    i = pl.program_id(1)
    j_below = jnp.maximum(i - 1, 0)
    n_far = jnp.maximum(i - 1, 0)

    def square(b):
        return jnp.concatenate([b, b], axis=1)

    def keys(j):
        return k_ref[pl.ds(pl.multiple_of(j * t, t), t), :]

    def finish(acc):
        lv = lamv_ref[...]
        lam = (jnp.exp(jnp.sum(lv[0:1] * lv[1:2], axis=-1, keepdims=True))
               - jnp.exp(jnp.sum(lv[2:3] * lv[3:4], axis=-1, keepdims=True)) + lam_init)
        o = acc[:2 * DHA] / acc[2 * DHA:2 * DHA + 1]
        od = o[:, :t] - lam * o[:, t:]
        y = od * lax.rsqrt(jnp.mean(od * od, axis=0, keepdims=True) + EPS) * subg_ref[...]
        return y * (1.0 - lam_init)

    s_diag = jnp.dot(keys(i), qbd_ref[...], preferred_element_type=F32) + square(bdiag_ref[...])
    s_below = jnp.dot(keys(j_below), qbd_ref[...], preferred_element_type=F32) + square(bcorn_ref[...])
    m = jnp.maximum(jnp.max(s_diag, axis=0, keepdims=True), jnp.max(s_below, axis=0, keepdims=True))
    offset_bf = (m + jnp.abs(m) * 2.0 ** -7).astype(BF16)
    offset = offset_bf.astype(F32)
    acc_ref[...] = (jnp.dot(vt_ref[i], jnp.exp2(s_diag - offset).astype(BF16), preferred_element_type=F32)
                    + jnp.dot(vt_ref[j_below], jnp.exp2(s_below - offset).astype(BF16),
                              preferred_element_type=F32))

    qx_ref[:2 * DHA, :] = qbd_ref[...]
    qx_ref[2 * DHA:2 * DHA + BF16_SUBLANES, :] = jnp.concatenate(
        [-offset_bf, jnp.full((1, 2 * t), NEG, BF16), jnp.zeros((BF16_SUBLANES - 2, 2 * t), BF16)], axis=0)
    qx_ref[2 * DHA + BF16_SUBLANES:, :] = jnp.zeros((QX_ROWS - 2 * DHA - BF16_SUBLANES, 2 * t), BF16)

    def far_p(j):
        kx = jnp.concatenate([keys(j), kx_ref[jnp.minimum(j, 1)]], axis=1)
        return jnp.exp2(jnp.dot(kx, qx_ref[...], preferred_element_type=F32)).astype(BF16)

    def far_pair(g, carry):
        pa = far_p(2 * g)
        pb = far_p(2 * g + 1)
        acc_ref[...] += (jnp.dot(vt_ref[2 * g], pa, preferred_element_type=F32)
                         + jnp.dot(vt_ref[2 * g + 1], pb, preferred_element_type=F32))
        return carry

    def far_single(g, carry):
        j = n_far - 1
        acc_ref[...] += jnp.dot(vt_ref[j], far_p(j), preferred_element_type=F32)
        return carry

    lax.fori_loop(0, n_far // 2, far_pair, 0)
    lax.fori_loop(0, n_far % 2, far_single, 0)

    y = finish(acc_ref[...])
    o_ref[...] = y.T.astype(o_ref.dtype)
    all_finite = jnp.min(jnp.where(jnp.abs(y) < jnp.finfo(F32).max, 1.0, 0.0))

    @pl.when(all_finite < 0.5)
    def _():
        m_ref[...] = jnp.full(m_ref.shape, NEG, F32)
        acc_ref[...] = jnp.zeros_like(acc_ref)

        def tile(j, bias):
            s = jnp.dot(keys(j), qbd_ref[...], preferred_element_type=F32)
            if bias is not None:
                s = s + bias
            m_prev = m_ref[...]
            m_new = jnp.maximum(m_prev, jnp.max(s, axis=0, keepdims=True))
            p = jnp.exp2(s - m_new).astype(BF16)
            acc_ref[...] = (jnp.exp2(m_prev - m_new) * acc_ref[...]
                            + jnp.dot(vt_ref[j], p, preferred_element_type=F32))
            m_ref[...] = m_new

        @pl.when(i >= 2)
        def _():
            tile(0, jnp.concatenate([bpad_ref[...]] * (2 * t // LANES), axis=1))

        def plain(j, carry):
            tile(j, None)
            return carry

        lax.fori_loop(1, i - 1, plain, 0)

        @pl.when(i >= 1)
        def _():
            tile(i - 1, square(bcorn_ref[...]))

        tile(i, square(bdiag_ref[...]))
        o_ref[...] = finish(acc_ref[...]).T.astype(o_ref.dtype)


def _attn_a(lamv, subg, qbd, zn, kx, vt, bdiag, bcorn, bpad, lam_init):
    t = ROW_TILE
    return pl.pallas_call(
        functools.partial(_attn_a_kernel, lam_init=lam_init),
        out_shape=jax.ShapeDtypeStruct((P_PAD, QA_W), BF16),
        grid=(HA, N_TILES),
        in_specs=[pl.BlockSpec((4, DHA), lambda h, i: (0, 0)),
                  pl.BlockSpec((2 * DHA, 1), lambda h, i: (0, 0)),
                  pl.BlockSpec((None, 2 * DHA, 2 * t), lambda h, i: (h, 0, i)),
                  pl.BlockSpec((P_PAD, 2 * DHA), lambda h, i: (0, h)),
                  pl.BlockSpec((2, t, LANES), lambda h, i: (0, 0, 0)),
                  pl.BlockSpec((None, N_TILES, VT_ROWS, t), lambda h, i: (h, 0, 0, 0)),
                  pl.BlockSpec((None, None, t, t), lambda h, i: (h, jnp.minimum(i, 1), 0, 0)),
                  pl.BlockSpec((None, None, t, t), lambda h, i: (h, jnp.minimum(i, 2), 0, 0)),
                  pl.BlockSpec((t, LANES), lambda h, i: (0, 0))],
        out_specs=pl.BlockSpec((t, 2 * DHA), lambda h, i: (i, h)),
        scratch_shapes=[pltpu.VMEM((QX_ROWS, 2 * t), BF16),
                        pltpu.VMEM((1, 2 * t), F32),
                        pltpu.VMEM((VT_ROWS, 2 * t), F32)],
        compiler_params=pltpu.CompilerParams(
            dimension_semantics=("parallel", "arbitrary"), vmem_limit_bytes=56 * MIB),
        name="diff_attention",
    )(lamv, subg, qbd, zn, kx, vt, bdiag, bcorn, bpad)


def _attn_b_kernel(q_ref, km_ref, kp_ref, kc_ref, vm_ref, vp_ref, vc_ref, bias_ref, sink_ref, o_ref):
    pairs = G_B // 2
    q = q_ref[...]
    q4 = jnp.concatenate([q[:, LANES * p:LANES * (p + 1)] for p in range(pairs)], axis=0)
    first = lax.broadcasted_iota(jnp.int32, (1, LANES), 1) < DHB

    def block_diag(m_ref, p_ref, c_ref):
        x = jnp.concatenate([m_ref[...], p_ref[...], c_ref[...]], axis=0)
        zero = jnp.zeros_like(x)
        return jnp.concatenate([jnp.where(first, x, zero), jnp.where(first, zero, x)], axis=0)

    s = lax.dot_general(q4, block_diag(km_ref, kp_ref, kc_ref), NT_DIMS, preferred_element_type=F32)
    s = s + bias_ref[...]
    probs = []
    for e in range(2):
        se = s[:, e * BAND:(e + 1) * BAND]
        sink = sink_ref[e]
        m = jnp.maximum(jnp.max(se, axis=-1, keepdims=True), sink)
        p = jnp.exp(se - m)
        denom = jnp.sum(p, axis=-1, keepdims=True) + jnp.exp(sink - m)
        probs.append((p / denom).astype(BF16))
    o4 = jnp.dot(jnp.concatenate(probs, axis=1), block_diag(vm_ref, vp_ref, vc_ref),
                 preferred_element_type=F32)
    o_ref[...] = jnp.concatenate([o4[BLK * p:BLK * (p + 1)] for p in range(pairs)], axis=1).astype(o_ref.dtype)


def _attn_b(zn, bias, sinks):
    q_blk = (QA_W + 0) // PAIR_ROWS
    k_blk = (QA_W + QB_W) // LANES
    v_blk = k_blk + HBKV

    def kv(col0, row):
        return pl.BlockSpec((BLK, LANES), lambda g, n: (row(n), col0 + g))

    meta, prev, cur = (lambda n: 0), (lambda n: jnp.maximum(n - 1, 0)), (lambda n: n)
    return pl.pallas_call(
        _attn_b_kernel,
        out_shape=jax.ShapeDtypeStruct((P_PAD, QB_W), BF16),
        grid=(HBKV, N_BLOCKS),
        in_specs=[pl.BlockSpec((BLK, PAIR_ROWS), lambda g, n: (n, q_blk + g)),
                  kv(k_blk, meta), kv(k_blk, prev), kv(k_blk, cur),
                  kv(v_blk, meta), kv(v_blk, prev), kv(v_blk, cur),
                  pl.BlockSpec((None, None, PAIR_ROWS, 2 * BAND), lambda g, n: (jnp.minimum(n, 2), g, 0, 0)),
                  pl.BlockSpec((None, 2, PAIR_ROWS, 1), lambda g, n: (g, 0, 0, 0))],
        out_specs=pl.BlockSpec((BLK, PAIR_ROWS), lambda g, n: (n, g)),
        compiler_params=pltpu.CompilerParams(dimension_semantics=("parallel", "parallel")),
        name="swa_sink_attention",
    )(zn, zn, zn, zn, zn, zn, zn, bias, sinks)


def _t5_bucket(dist):
    n = jnp.maximum(dist, 0)
    max_exact = N_BUCKETS // 2
    nf = jnp.maximum(n, 1).astype(F32)
    large = max_exact + (jnp.log(nf / max_exact) / math.log(MAX_DIST / max_exact)
                         * (N_BUCKETS - max_exact)).astype(jnp.int32)
    large = jnp.minimum(large, N_BUCKETS - 1)
    return jnp.where(n < max_exact, n, large)


def _bias_tiles_a(rel_a):
    t = ROW_TILE
    span = 2 * t
    table = rel_a[_t5_bucket(jnp.arange(span))].astype(F32) - rel_a[N_BUCKETS - 1].astype(F32)
    table = table.T * LOG2E

    def toeplitz(v):
        flat = jnp.broadcast_to(v[:, None, :], (HA, t, span)).reshape(HA, t * span)
        return flat[:, :t * (span - 1)].reshape(HA, t, span - 1)[:, :, :t]

    kb = jnp.arange(t)[:, None]
    qa = jnp.arange(t)[None, :]
    pad = (kb < PAD_FRONT)[None]
    causal = jnp.where((qa >= kb)[None], toeplitz(table), NEG)
    corner = toeplitz(jnp.roll(table, -t, axis=1))
    diag = jnp.stack([jnp.where(pad, NEG, causal), causal], axis=1)
    corner = jnp.stack([jnp.full_like(corner, NEG), jnp.where(pad, NEG, corner), corner], axis=1)
    pad_col = jnp.broadcast_to(jnp.where(pad[0], NEG, 0.0).astype(F32), (t, LANES))
    lane = jnp.arange(LANES)[None, :]
    ones_col = (lane == 0)
    kx = jnp.stack([jnp.where(ones_col | ((lane == 1) & pad[0]), 1.0, 0.0),
                    jnp.broadcast_to(jnp.where(ones_col, 1.0, 0.0), (t, LANES))]).astype(BF16)
    return diag, corner, pad_col, kx


def _bias_tiles_b(rel_b):
    table = rel_b[_t5_bucket(jnp.arange(BAND))].astype(F32)
    a = jnp.arange(BLK)[:, None]
    b = jnp.arange(BLK)[None, :]
    is_meta = b >= PAD_FRONT

    def look(dist, ok):
        return jnp.where(ok[None], jnp.moveaxis(table[jnp.clip(dist, 0, BAND - 1)], -1, 0), NEG)

    none = jnp.zeros((BLK, BLK), bool)
    prev = look(BLK + a - b, b > a)
    cur = look(a - b, a >= b)
    masked = look(a - b, none)
    blocks = []
    for n in range(3):
        dist_m = n * BLK + a - b
        meta = look(dist_m, is_meta & (dist_m >= 0))
        blocks.append(jnp.concatenate([meta, prev if n >= 2 else masked, cur if n >= 1 else masked], axis=-1))
    per_head = jnp.stack(blocks, axis=0)
    pairs = G_B // 2
    stacked = per_head.reshape(3, HBKV, pairs, 2, BLK, BAND).transpose(0, 1, 2, 4, 3, 5)
    return stacked.reshape(3, HBKV, PAIR_ROWS, 2 * BAND)


def _sinks_b(sinks):
    pairs = G_B // 2
    s = sinks.astype(F32).reshape(HBKV, pairs, 2).transpose(0, 2, 1)
    return jnp.repeat(s, BLK, axis=2)[..., None]


def kernel(x, meta_tokens, rel_bias, w_in, w_branch_a, w_branch_b, w_out, lambda_q1, lambda_k1,
           lambda_q2, lambda_k2, diff_norm, sinks, norm_attn_pre, norm_attn_post, w_ff1, w_ff2,
           norm_ff_pre, norm_ff_post):
    assert x.shape == (1, SEQ, D_MODEL)
    h = jnp.concatenate([jnp.zeros((PAD_FRONT, D_MODEL), F32), meta_tokens.astype(F32), x[0],
                         jnp.zeros((P_PAD - P_STREAM, D_MODEL), F32)], axis=0)
    bdiag, bcorn, bpad, kx = _bias_tiles_a(rel_bias[:, :HA])
    bias_b = _bias_tiles_b(rel_bias[:, HA:HA + HBQ])

    u = _rmsnorm_rows(h, norm_attn_pre[0])
    for l in range(DEPTH):
        lam_init = 0.8 - 0.6 * math.exp(-0.3 * l)
        w = w_in[l]
        wqt = (w[:, :QA_W] * (DHA ** -0.5 * LOG2E)).T.astype(BF16)
        wvt = w[:, 2 * QA_W:3 * QA_W].T.astype(BF16)
        kv_b = w[:, 3 * QA_W + QB_W:GATE_OFF].reshape(D_MODEL, 2 * HBKV, 1, DHB)
        kv_b = jnp.broadcast_to(kv_b, (D_MODEL, 2 * HBKV, 2, DHB)).reshape(D_MODEL, 4 * HBKV * DHB)
        wn = jnp.concatenate([w[:, QA_W:2 * QA_W], w[:, 3 * QA_W:3 * QA_W + QB_W] * DHB ** -0.5, kv_b],
                             axis=1).astype(BF16)
        qbd, vt, zn = _proj(u, wqt, wvt, wn)

        lamv = jnp.stack([lambda_q1[l], lambda_k1[l], lambda_q2[l], lambda_k2[l]]).astype(F32)
        ya = _attn_a(lamv, diff_norm[l].astype(F32).reshape(2 * DHA, 1), qbd, zn, kx, vt,
                     bdiag, bcorn, bpad, lam_init)
        yb = _attn_b(zn, bias_b, _sinks_b(sinks[l]))

        h, u_ff = _merge(u, ya, yb, h, w[:, GATE_OFF:].astype(BF16), w_branch_a[l].astype(BF16),
                         w_branch_b[l].astype(BF16), w_out[l].astype(BF16), norm_attn_post[l], norm_ff_pre[l])
        h, u = _ffn(u_ff, w_ff1[l].astype(BF16), w_ff2[l].astype(BF16), h, norm_ff_post[l],
                    norm_attn_pre[(l + 1) % DEPTH])
    return h[BLK:BLK + SEQ][None]
```

```python
import functools
import math

import jax
import jax.numpy as jnp
from jax import lax
from jax.experimental import pallas as pl
from jax.experimental.pallas import tpu as pltpu

D_MODEL = 2048
SEQ = 16384
DEPTH = 4
N_META = 16
BLK = 128
WINDOW = 128
HA = 8
DHA = 64
HBQ = 16
HBKV = 2
DHB = 64
G_B = HBQ // HBKV
D_FF = 4 * D_MODEL
N_BUCKETS = 32
MAX_DIST = 128
EPS = 1e-6
NEG = -1e30
LOG2E = math.log2(math.e)

QA_W = HA * 2 * DHA
QB_W = HBQ * DHB
GATE_OFF = 3 * QA_W + QB_W + 2 * HBKV * DHB
PAD_FRONT = BLK - N_META
P_STREAM = SEQ + BLK

LANES = 128
BF16_SUBLANES = 16
ROW_TILE = 512
P_PAD = -(-P_STREAM // ROW_TILE) * ROW_TILE
N_TILES = P_PAD // ROW_TILE
N_BLOCKS = P_PAD // BLK
VT_ROWS = 2 * DHA + BF16_SUBLANES
QX_ROWS = 2 * LANES
PAIR_ROWS = (G_B // 2) * BLK
BAND = 3 * BLK
NAT_W = QA_W + 2 * HBKV * DHB
TB_ROWS = QB_W + 2 * HBKV * DHB
FAR_GROUP = 8
MERGE_TILE = 256
FF_TILE = 1024
MIB = 1024 * 1024

F32 = jnp.float32
BF16 = jnp.bfloat16
NT_DIMS = (((1,), (1,)), ((), ()))


def _rms(x, g):
    return x * lax.rsqrt(jnp.mean(x * x, axis=-1, keepdims=True) + EPS) * g


def _rmsnorm_kernel(h_ref, g_ref, o_ref):
    o_ref[...] = _rms(h_ref[...], g_ref[...]).astype(o_ref.dtype)


def _rmsnorm_rows(h, g):
    return pl.pallas_call(
        _rmsnorm_kernel,
        out_shape=jax.ShapeDtypeStruct(h.shape, BF16),
        grid=(N_TILES,),
        in_specs=[pl.BlockSpec((ROW_TILE, D_MODEL), lambda i: (i, 0)),
                  pl.BlockSpec((1, D_MODEL), lambda i: (0, 0))],
        out_specs=pl.BlockSpec((ROW_TILE, D_MODEL), lambda i: (i, 0)),
        compiler_params=pltpu.CompilerParams(dimension_semantics=("parallel",)),
        name="rmsnorm_rows",
    )(h, g.reshape(1, D_MODEL))


def _proj_kernel(u_ref, wqt_ref, wvt_ref, wtb_ref, wn_ref, qbd_ref, vt_ref, tb_ref, zn_ref):
    t = ROW_TILE
    u = u_ref[...]
    qt = lax.dot_general(wqt_ref[...], u, NT_DIMS, preferred_element_type=F32).reshape(HA, 2 * DHA, t)
    top = lax.broadcasted_iota(jnp.int32, (1, 2 * DHA, 1), 1) < DHA
    zero = jnp.zeros_like(qt)
    qbd_ref[...] = jnp.concatenate([jnp.where(top, qt, zero), jnp.where(top, zero, qt)], axis=2).astype(BF16)
    vt = lax.dot_general(wvt_ref[...], u, NT_DIMS, preferred_element_type=F32).reshape(HA, 2 * DHA, t)
    vt_ref[:, :2 * DHA, :] = vt.astype(BF16)
    vt_ref[:, 2 * DHA:, :] = jnp.ones((HA, BF16_SUBLANES, t), BF16)
    tb_ref[...] = lax.dot_general(wtb_ref[...], u, NT_DIMS, preferred_element_type=F32).astype(BF16)
    zn_ref[...] = jnp.dot(u, wn_ref[...], preferred_element_type=F32).astype(BF16)


def _proj(u, wqt, wvt, wtb, wn):
    t = ROW_TILE
    once = pl.Buffered(1)
    return pl.pallas_call(
        _proj_kernel,
        out_shape=(jax.ShapeDtypeStruct((HA, 2 * DHA, N_TILES * 2 * t), BF16),
                   jax.ShapeDtypeStruct((HA, N_TILES, VT_ROWS, t), BF16),
                   jax.ShapeDtypeStruct((TB_ROWS, P_PAD), BF16),
                   jax.ShapeDtypeStruct((P_PAD, NAT_W), BF16)),
        grid=(N_TILES,),
        in_specs=[pl.BlockSpec((t, D_MODEL), lambda i: (i, 0)),
                  pl.BlockSpec((QA_W, D_MODEL), lambda i: (0, 0), pipeline_mode=once),
                  pl.BlockSpec((QA_W, D_MODEL), lambda i: (0, 0), pipeline_mode=once),
                  pl.BlockSpec((TB_ROWS, D_MODEL), lambda i: (0, 0), pipeline_mode=once),
                  pl.BlockSpec((D_MODEL, NAT_W), lambda i: (0, 0), pipeline_mode=once)],
        out_specs=(pl.BlockSpec((HA, 2 * DHA, 2 * t), lambda i: (0, 0, i)),
                   pl.BlockSpec((HA, None, VT_ROWS, t), lambda i: (0, i, 0, 0)),
                   pl.BlockSpec((TB_ROWS, t), lambda i: (0, i)),
                   pl.BlockSpec((t, NAT_W), lambda i: (i, 0))),
        compiler_params=pltpu.CompilerParams(
            dimension_semantics=("parallel",), vmem_limit_bytes=56 * MIB),
        name="proj_mixers",
    )(u, wqt, wvt, wtb, wn)


def _merge_kernel(u_ref, ya_ref, yb_ref, h_ref, wg_ref, wa_ref, wb_ref, wo_ref,
                  gpost_ref, gnext_ref, h_out_ref, u_out_ref):
    u = u_ref[...]
    gate_a = jax.nn.sigmoid(jnp.dot(u, wg_ref[:, :D_MODEL], preferred_element_type=F32))
    mix = gate_a * jnp.dot(ya_ref[...], wa_ref[...], preferred_element_type=F32)
    gate_b = jax.nn.sigmoid(jnp.dot(u, wg_ref[:, D_MODEL:], preferred_element_type=F32))
    mix = mix + gate_b * jnp.dot(yb_ref[...], wb_ref[...], preferred_element_type=F32)
    t = jnp.dot(mix.astype(BF16), wo_ref[...], preferred_element_type=F32)
    hn = h_ref[...] + _rms(t, gpost_ref[...])
    h_out_ref[...] = hn
    u_out_ref[...] = _rms(hn, gnext_ref[...]).astype(BF16)


def _merge(u, ya, yb, h, wg, wa, wb, wo, g_post, g_next):
    tm = MERGE_TILE
    once = pl.Buffered(1)
    return pl.pallas_call(
        _merge_kernel,
        out_shape=(jax.ShapeDtypeStruct((P_PAD, D_MODEL), F32),
                   jax.ShapeDtypeStruct((P_PAD, D_MODEL), BF16)),
        grid=(P_PAD // tm,),
        in_specs=[pl.BlockSpec((tm, D_MODEL), lambda i: (i, 0)),
                  pl.BlockSpec((tm, QA_W), lambda i: (i, 0)),
                  pl.BlockSpec((tm, QB_W), lambda i: (i, 0)),
                  pl.BlockSpec((tm, D_MODEL), lambda i: (i, 0)),
                  pl.BlockSpec((D_MODEL, 2 * D_MODEL), lambda i: (0, 0), pipeline_mode=once),
                  pl.BlockSpec((QA_W, D_MODEL), lambda i: (0, 0), pipeline_mode=once),
                  pl.BlockSpec((QB_W, D_MODEL), lambda i: (0, 0), pipeline_mode=once),
                  pl.BlockSpec((D_MODEL, D_MODEL), lambda i: (0, 0), pipeline_mode=once),
                  pl.BlockSpec((1, D_MODEL), lambda i: (0, 0)),
                  pl.BlockSpec((1, D_MODEL), lambda i: (0, 0))],
        out_specs=(pl.BlockSpec((tm, D_MODEL), lambda i: (i, 0)),
                   pl.BlockSpec((tm, D_MODEL), lambda i: (i, 0))),
        compiler_params=pltpu.CompilerParams(
            dimension_semantics=("parallel",), vmem_limit_bytes=60 * MIB),
        name="gated_merge",
    )(u, ya, yb, h, wg, wa, wb, wo, g_post.reshape(1, D_MODEL), g_next.reshape(1, D_MODEL))


def _ffn_kernel(u_ref, w1_ref, w2_ref, h_ref, gpost_ref, gnext_ref, h_out_ref, u_out_ref, acc_ref):
    f = pl.program_id(1)

    @pl.when(f == 0)
    def _():
        acc_ref[...] = jnp.zeros_like(acc_ref)

    a = jnp.dot(u_ref[...], w1_ref[...], preferred_element_type=F32)
    a = jnp.square(jnp.maximum(a, 0.0))
    acc_ref[...] += jnp.dot(a.astype(BF16), w2_ref[...], preferred_element_type=F32)

    @pl.when(f == pl.num_programs(1) - 1)
    def _():
        hn = h_ref[...] + _rms(acc_ref[...], gpost_ref[...])
        h_out_ref[...] = hn
        u_out_ref[...] = _rms(hn, gnext_ref[...]).astype(BF16)


def _ffn(u, w1, w2, h, g_post, g_next):
    tm, tf = ROW_TILE, FF_TILE
    return pl.pallas_call(
        _ffn_kernel,
        out_shape=(jax.ShapeDtypeStruct((P_PAD, D_MODEL), F32),
                   jax.ShapeDtypeStruct((P_PAD, D_MODEL), BF16)),
        grid=(N_TILES, D_FF // tf),
        in_specs=[pl.BlockSpec((tm, D_MODEL), lambda i, f: (i, 0)),
                  pl.BlockSpec((D_MODEL, tf), lambda i, f: (0, f)),
                  pl.BlockSpec((tf, D_MODEL), lambda i, f: (f, 0)),
                  pl.BlockSpec((tm, D_MODEL), lambda i, f: (i, 0)),
                  pl.BlockSpec((1, D_MODEL), lambda i, f: (0, 0)),
                  pl.BlockSpec((1, D_MODEL), lambda i, f: (0, 0))],
        out_specs=(pl.BlockSpec((tm, D_MODEL), lambda i, f: (i, 0)),
                   pl.BlockSpec((tm, D_MODEL), lambda i, f: (i, 0))),
        scratch_shapes=[pltpu.VMEM((tm, D_MODEL), F32)],
        compiler_params=pltpu.CompilerParams(
            dimension_semantics=("parallel", "arbitrary"), vmem_limit_bytes=56 * MIB),
        name="ffn_relu2",
    )(u, w1, w2, h, g_post.reshape(1, D_MODEL), g_next.reshape(1, D_MODEL))


def _attn_a_kernel(lamv_ref, subg_ref, qbd_ref, k_ref, kx_ref, vt_ref, bdiag_ref, bcorn_ref, bpad_ref,
                   o_ref, qx_ref, m_ref, acc_ref, *, lam_init):
    t = ROW_TILE
    i = pl.program_id(1)
    j_below = jnp.maximum(i - 1, 0)
    n_far = jnp.maximum(i - 1, 0)

    def square(b):
        return jnp.concatenate([b, b], axis=1)

    def keys(j):
        return k_ref[pl.ds(pl.multiple_of(j * t, t), t), :]

    def finish(acc):
        lv = lamv_ref[...]
        lam = (jnp.exp(jnp.sum(lv[0:1] * lv[1:2], axis=-1, keepdims=True))
               - jnp.exp(jnp.sum(lv[2:3] * lv[3:4], axis=-1, keepdims=True)) + lam_init)
        o = acc[:2 * DHA] / acc[2 * DHA:2 * DHA + 1]
        od = o[:, :t] - lam * o[:, t:]
        y = od * lax.rsqrt(jnp.mean(od * od, axis=0, keepdims=True) + EPS) * subg_ref[...]
        return y * (1.0 - lam_init)

    s_diag = jnp.dot(keys(i), qbd_ref[...], preferred_element_type=F32) + square(bdiag_ref[...])
    s_below = jnp.dot(keys(j_below), qbd_ref[...], preferred_element_type=F32) + square(bcorn_ref[...])
    m = jnp.maximum(jnp.max(s_diag, axis=0, keepdims=True), jnp.max(s_below, axis=0, keepdims=True))
    offset_bf = (m + jnp.abs(m) * 2.0 ** -7).astype(BF16)
    offset = offset_bf.astype(F32)
    acc_ref[...] = (jnp.dot(vt_ref[i], jnp.exp2(s_diag - offset).astype(BF16), preferred_element_type=F32)
                    + jnp.dot(vt_ref[j_below], jnp.exp2(s_below - offset).astype(BF16),
                              preferred_element_type=F32))

    qx_ref[:2 * DHA, :] = qbd_ref[...]
    qx_ref[2 * DHA:2 * DHA + BF16_SUBLANES, :] = jnp.concatenate(
        [-offset_bf, jnp.full((1, 2 * t), NEG, BF16), jnp.zeros((BF16_SUBLANES - 2, 2 * t), BF16)], axis=0)
    qx_ref[2 * DHA + BF16_SUBLANES:, :] = jnp.zeros((QX_ROWS - 2 * DHA - BF16_SUBLANES, 2 * t), BF16)

    def far_p(j):
        kx = jnp.concatenate([keys(j), kx_ref[jnp.minimum(j, 1)]], axis=1)
        return jnp.exp2(jnp.dot(kx, qx_ref[...], preferred_element_type=F32)).astype(BF16)

    def far_group(width):
        def body(g, first):
            ps = [far_p(first + w) for w in range(width)]
            pv = jnp.dot(vt_ref[first], ps[0], preferred_element_type=F32)
            for w in range(1, width):
                pv = pv + jnp.dot(vt_ref[first + w], ps[w], preferred_element_type=F32)
            acc_ref[...] += pv
            return first + width
        return body

    done = lax.fori_loop(0, n_far // FAR_GROUP, far_group(FAR_GROUP), jnp.int32(0))
    width = FAR_GROUP // 2
    while width >= 1:
        done = lax.fori_loop(0, (n_far % (2 * width)) // width, far_group(width), done)
        width //= 2

    y = finish(acc_ref[...])
    o_ref[...] = y.T.astype(o_ref.dtype)
    all_finite = jnp.min(jnp.where(jnp.abs(y) < jnp.finfo(F32).max, 1.0, 0.0))

    @pl.when(all_finite < 0.5)
    def _():
        m_ref[...] = jnp.full(m_ref.shape, NEG, F32)
        acc_ref[...] = jnp.zeros_like(acc_ref)

        def tile(j, bias):
            s = jnp.dot(keys(j), qbd_ref[...], preferred_element_type=F32)
            if bias is not None:
                s = s + bias
            m_prev = m_ref[...]
            m_new = jnp.maximum(m_prev, jnp.max(s, axis=0, keepdims=True))
            p = jnp.exp2(s - m_new).astype(BF16)
            acc_ref[...] = (jnp.exp2(m_prev - m_new) * acc_ref[...]
                            + jnp.dot(vt_ref[j], p, preferred_element_type=F32))
            m_ref[...] = m_new

        @pl.when(i >= 2)
        def _():
            tile(0, jnp.concatenate([bpad_ref[...]] * (2 * t // LANES), axis=1))

        def plain(j, carry):
            tile(j, None)
            return carry

        lax.fori_loop(1, i - 1, plain, 0)

        @pl.when(i >= 1)
        def _():
            tile(i - 1, square(bcorn_ref[...]))

        tile(i, square(bdiag_ref[...]))
        o_ref[...] = finish(acc_ref[...]).T.astype(o_ref.dtype)


def _attn_a(lamv, subg, qbd, zn, kx, vt, bdiag, bcorn, bpad, lam_init):
    t = ROW_TILE
    return pl.pallas_call(
        functools.partial(_attn_a_kernel, lam_init=lam_init),
        out_shape=jax.ShapeDtypeStruct((P_PAD, QA_W), BF16),
        grid=(HA, N_TILES),
        in_specs=[pl.BlockSpec((4, DHA), lambda h, i: (0, 0)),
                  pl.BlockSpec((2 * DHA, 1), lambda h, i: (0, 0)),
                  pl.BlockSpec((None, 2 * DHA, 2 * t), lambda h, i: (h, 0, i)),
                  pl.BlockSpec((P_PAD, 2 * DHA), lambda h, i: (0, h)),
                  pl.BlockSpec((2, t, LANES), lambda h, i: (0, 0, 0)),
                  pl.BlockSpec((None, N_TILES, VT_ROWS, t), lambda h, i: (h, 0, 0, 0)),
                  pl.BlockSpec((None, None, t, t), lambda h, i: (h, jnp.minimum(i, 1), 0, 0)),
                  pl.BlockSpec((None, None, t, t), lambda h, i: (h, jnp.minimum(i, 2), 0, 0)),
                  pl.BlockSpec((t, LANES), lambda h, i: (0, 0))],
        out_specs=pl.BlockSpec((t, 2 * DHA), lambda h, i: (i, h)),
        scratch_shapes=[pltpu.VMEM((QX_ROWS, 2 * t), BF16),
                        pltpu.VMEM((1, 2 * t), F32),
                        pltpu.VMEM((VT_ROWS, 2 * t), F32)],
        compiler_params=pltpu.CompilerParams(
            dimension_semantics=("parallel", "arbitrary"), vmem_limit_bytes=56 * MIB),
        name="diff_attention",
    )(lamv, subg, qbd, zn, kx, vt, bdiag, bcorn, bpad)


def _attn_b_kernel(q_ref, km_ref, kp_ref, kc_ref, vm_ref, vp_ref, vc_ref, bias_ref, sink_ref, o_ref):
    pairs = G_B // 2
    qt = q_ref[...]
    rhs = jnp.concatenate([qt[LANES * p:LANES * (p + 1)] for p in range(pairs)], axis=1)
    kk = jnp.concatenate([km_ref[...], kp_ref[...], kc_ref[...]], axis=0)
    first = lax.broadcasted_iota(jnp.int32, (1, LANES), 1) < DHB
    kk = jnp.concatenate([jnp.where(first, kk, jnp.zeros_like(kk)),
                          jnp.where(first, jnp.zeros_like(kk), kk)], axis=0)
    s = jnp.dot(kk, rhs, preferred_element_type=F32) + bias_ref[...]

    vt = jnp.concatenate([vm_ref[...], vp_ref[...], vc_ref[...]], axis=1)
    top = lax.broadcasted_iota(jnp.int32, (LANES, 1), 0) < DHB
    vt = jnp.concatenate([jnp.where(top, vt, jnp.zeros_like(vt)),
                          jnp.where(top, jnp.zeros_like(vt), vt)], axis=1)
    row = lax.broadcasted_iota(jnp.int32, (BF16_SUBLANES, 2 * BAND), 0)
    col = lax.broadcasted_iota(jnp.int32, (BF16_SUBLANES, 2 * BAND), 1)
    ones = jnp.where(row == col // BAND, 1.0, 0.0).astype(BF16)

    probs, maxes = [], []
    for e in range(2):
        se = s[e * BAND:(e + 1) * BAND]
        m = jnp.maximum(jnp.max(se, axis=0, keepdims=True), sink_ref[e])
        probs.append(jnp.exp(se - m).astype(BF16))
        maxes.append(m)
    acc = jnp.dot(jnp.concatenate([vt, ones], axis=0), jnp.concatenate(probs, axis=0),
                  preferred_element_type=F32)
    denom = [acc[2 * DHB + e:2 * DHB + e + 1] + jnp.exp(sink_ref[e] - maxes[e]) for e in range(2)]
    o = acc[:2 * DHB] / jnp.where(top, denom[0], denom[1])
    o_ref[...] = jnp.concatenate([o[:, BLK * p:BLK * (p + 1)].T for p in range(pairs)],
                                 axis=1).astype(o_ref.dtype)


def _attn_b(zn, tb, bias, sinks):
    k_blk = QA_W // LANES
    v_blk = QB_W // LANES
    meta, prev, cur = (lambda n: 0), (lambda n: jnp.maximum(n - 1, 0)), (lambda n: n)

    def k_spec(row):
        return pl.BlockSpec((BLK, LANES), lambda g, n: (row(n), k_blk + g))

    def v_spec(row):
        return pl.BlockSpec((LANES, BLK), lambda g, n: (v_blk + g, row(n)))

    return pl.pallas_call(
        _attn_b_kernel,
        out_shape=jax.ShapeDtypeStruct((P_PAD, QB_W), BF16),
        grid=(HBKV, N_BLOCKS),
        in_specs=[pl.BlockSpec((PAIR_ROWS, BLK), lambda g, n: (g, n)),
                  k_spec(meta), k_spec(prev), k_spec(cur),
                  v_spec(meta), v_spec(prev), v_spec(cur),
                  pl.BlockSpec((None, None, 2 * BAND, PAIR_ROWS), lambda g, n: (jnp.minimum(n, 2), g, 0, 0)),
                  pl.BlockSpec((None, 2, 1, PAIR_ROWS), lambda g, n: (g, 0, 0, 0))],
        out_specs=pl.BlockSpec((BLK, PAIR_ROWS), lambda g, n: (n, g)),
        compiler_params=pltpu.CompilerParams(dimension_semantics=("parallel", "parallel")),
        name="swa_sink_attention",
    )(tb, zn, zn, zn, tb, tb, tb, bias, sinks)


def _t5_bucket(dist):
    n = jnp.maximum(dist, 0)
    max_exact = N_BUCKETS // 2
    nf = jnp.maximum(n, 1).astype(F32)
    large = max_exact + (jnp.log(nf / max_exact) / math.log(MAX_DIST / max_exact)
                         * (N_BUCKETS - max_exact)).astype(jnp.int32)
    large = jnp.minimum(large, N_BUCKETS - 1)
    return jnp.where(n < max_exact, n, large)


def _bias_tiles_a(rel_a):
    t = ROW_TILE
    span = 2 * t
    table = rel_a[_t5_bucket(jnp.arange(span))].astype(F32) - rel_a[N_BUCKETS - 1].astype(F32)
    table = table.T * LOG2E

    def toeplitz(v):
        flat = jnp.broadcast_to(v[:, None, :], (HA, t, span)).reshape(HA, t * span)
        return flat[:, :t * (span - 1)].reshape(HA, t, span - 1)[:, :, :t]

    kb = jnp.arange(t)[:, None]
    qa = jnp.arange(t)[None, :]
    pad = (kb < PAD_FRONT)[None]
    causal = jnp.where((qa >= kb)[None], toeplitz(table), NEG)
    corner = toeplitz(jnp.roll(table, -t, axis=1))
    diag = jnp.stack([jnp.where(pad, NEG, causal), causal], axis=1)
    corner = jnp.stack([jnp.full_like(corner, NEG), jnp.where(pad, NEG, corner), corner], axis=1)
    pad_col = jnp.broadcast_to(jnp.where(pad[0], NEG, 0.0).astype(F32), (t, LANES))
    lane = jnp.arange(LANES)[None, :]
    ones_col = (lane == 0)
    kx = jnp.stack([jnp.where(ones_col | ((lane == 1) & pad[0]), 1.0, 0.0),
                    jnp.broadcast_to(jnp.where(ones_col, 1.0, 0.0), (t, LANES))]).astype(BF16)
    return diag, corner, pad_col, kx


def _bias_tiles_b(rel_b):
    table = rel_b[_t5_bucket(jnp.arange(BAND))].astype(F32)
    a = jnp.arange(BLK)[:, None]
    b = jnp.arange(BLK)[None, :]
    is_meta = b >= PAD_FRONT

    def look(dist, ok):
        return jnp.where(ok[None], jnp.moveaxis(table[jnp.clip(dist, 0, BAND - 1)], -1, 0), NEG)

    none = jnp.zeros((BLK, BLK), bool)
    prev = look(BLK + a - b, b > a)
    cur = look(a - b, a >= b)
    masked = look(a - b, none)
    blocks = []
    for n in range(3):
        dist_m = n * BLK + a - b
        meta = look(dist_m, is_meta & (dist_m >= 0))
        blocks.append(jnp.concatenate([meta, prev if n >= 2 else masked, cur if n >= 1 else masked], axis=-1))
    per_head = jnp.stack(blocks, axis=0)
    pairs = G_B // 2
    stacked = per_head.reshape(3, HBKV, pairs, 2, BLK, BAND).transpose(0, 1, 3, 5, 2, 4)
    return stacked.reshape(3, HBKV, 2 * BAND, PAIR_ROWS)


def _sinks_b(sinks):
    pairs = G_B // 2
    s = sinks.astype(F32).reshape(HBKV, pairs, 2).transpose(0, 2, 1)
    return jnp.repeat(s, BLK, axis=2)[:, :, None, :]


def kernel(x, meta_tokens, rel_bias, w_in, w_branch_a, w_branch_b, w_out, lambda_q1, lambda_k1,
           lambda_q2, lambda_k2, diff_norm, sinks, norm_attn_pre, norm_attn_post, w_ff1, w_ff2,
           norm_ff_pre, norm_ff_post):
    assert x.shape == (1, SEQ, D_MODEL)
    h = jnp.concatenate([jnp.zeros((PAD_FRONT, D_MODEL), F32), meta_tokens.astype(F32), x[0],
                         jnp.zeros((P_PAD - P_STREAM, D_MODEL), F32)], axis=0)
    bdiag, bcorn, bpad, kx = _bias_tiles_a(rel_bias[:, :HA])
    bias_b = _bias_tiles_b(rel_bias[:, HA:HA + HBQ])

    u = _rmsnorm_rows(h, norm_attn_pre[0])
    for l in range(DEPTH):
        lam_init = 0.8 - 0.6 * math.exp(-0.3 * l)
        w = w_in[l]
        wqt = (w[:, :QA_W] * (DHA ** -0.5 * LOG2E)).T.astype(BF16)
        wvt = w[:, 2 * QA_W:3 * QA_W].T.astype(BF16)
        kv_b = w[:, 3 * QA_W + QB_W:GATE_OFF].reshape(D_MODEL, 2 * HBKV, 1, DHB)
        kv_b = jnp.broadcast_to(kv_b, (D_MODEL, 2 * HBKV, 2, DHB)).reshape(D_MODEL, 2, 2 * HBKV * DHB)
        wn = jnp.concatenate([w[:, QA_W:2 * QA_W], kv_b[:, 0]], axis=1).astype(BF16)
        wtb = jnp.concatenate([w[:, 3 * QA_W:3 * QA_W + QB_W] * DHB ** -0.5, kv_b[:, 1]], axis=1).T.astype(BF16)
        qbd, vt, tb, zn = _proj(u, wqt, wvt, wtb, wn)

        lamv = jnp.stack([lambda_q1[l], lambda_k1[l], lambda_q2[l], lambda_k2[l]]).astype(F32)
        ya = _attn_a(lamv, diff_norm[l].astype(F32).reshape(2 * DHA, 1), qbd, zn, kx, vt,
                     bdiag, bcorn, bpad, lam_init)
        yb = _attn_b(zn, tb, bias_b, _sinks_b(sinks[l]))

        h, u_ff = _merge(u, ya, yb, h, w[:, GATE_OFF:].astype(BF16), w_branch_a[l].astype(BF16),
                         w_branch_b[l].astype(BF16), w_out[l].astype(BF16), norm_attn_post[l], norm_ff_pre[l])
        h, u = _ffn(u_ff, w_ff1[l].astype(BF16), w_ff2[l].astype(BF16), h, norm_ff_post[l],
                    norm_attn_pre[(l + 1) % DEPTH])
    return h[BLK:BLK + SEQ][None]
```

```python
import functools
import math

import jax
import jax.numpy as jnp
from jax import lax
from jax.experimental import pallas as pl
from jax.experimental.pallas import tpu as pltpu

D_MODEL = 2048
SEQ = 16384
DEPTH = 4
N_META = 16
BLK = 128
WINDOW = 128
HA = 8
DHA = 64
HBQ = 16
HBKV = 2
DHB = 64
G_B = HBQ // HBKV
D_FF = 4 * D_MODEL
N_BUCKETS = 32
MAX_DIST = 128
EPS = 1e-6
NEG = -1e30
LOG2E = math.log2(math.e)

QA_W = HA * 2 * DHA
QB_W = HBQ * DHB
GATE_OFF = 3 * QA_W + QB_W + 2 * HBKV * DHB
PAD_FRONT = BLK - N_META
P_STREAM = SEQ + BLK

LANES = 128
BF16_SUBLANES = 16
ROW_TILE = 512
P_PAD = -(-P_STREAM // ROW_TILE) * ROW_TILE
N_TILES = P_PAD // ROW_TILE
N_BLOCKS = P_PAD // BLK
MAIN_ROWS = (N_TILES - 1) * ROW_TILE
LAST_ROWS = P_STREAM - MAIN_ROWS
VT_ROWS = 2 * DHA + BF16_SUBLANES
QX_ROWS = 2 * LANES
PAIR_ROWS = (G_B // 2) * BLK
BAND = 3 * BLK
NAT_W = QA_W + 2 * HBKV * DHB
TB_ROWS = QB_W + 2 * HBKV * DHB
FAR_GROUP = 8
MERGE_TILE = 256
FF_TILE = 1024
MIB = 1024 * 1024

F32 = jnp.float32
BF16 = jnp.bfloat16
NT_DIMS = (((1,), (1,)), ((), ()))


def _rms(x, g):
    return x * lax.rsqrt(jnp.mean(x * x, axis=-1, keepdims=True) + EPS) * g


def _rmsnorm_kernel(h_ref, g_ref, o_ref):
    o_ref[...] = _rms(h_ref[...], g_ref[...]).astype(o_ref.dtype)


def _rmsnorm_rows(h, g):
    return pl.pallas_call(
        _rmsnorm_kernel,
        out_shape=jax.ShapeDtypeStruct(h.shape, BF16),
        grid=(N_TILES,),
        in_specs=[pl.BlockSpec((ROW_TILE, D_MODEL), lambda i: (i, 0)),
                  pl.BlockSpec((1, D_MODEL), lambda i: (0, 0))],
        out_specs=pl.BlockSpec((ROW_TILE, D_MODEL), lambda i: (i, 0)),
        compiler_params=pltpu.CompilerParams(dimension_semantics=("parallel",)),
        name="rmsnorm_rows",
    )(h, g.reshape(1, D_MODEL))


def _proj_kernel(u_ref, wqt_ref, wvt_ref, wtb_ref, wn_ref, qbd_ref, vt_ref, tb_ref, zn_ref):
    t = ROW_TILE
    u = u_ref[...]
    qt = lax.dot_general(wqt_ref[...], u, NT_DIMS, preferred_element_type=F32).reshape(HA, 2 * DHA, t)
    top = lax.broadcasted_iota(jnp.int32, (1, 2 * DHA, 1), 1) < DHA
    zero = jnp.zeros_like(qt)
    qbd_ref[...] = jnp.concatenate([jnp.where(top, qt, zero), jnp.where(top, zero, qt)], axis=2).astype(BF16)
    vt = lax.dot_general(wvt_ref[...], u, NT_DIMS, preferred_element_type=F32).reshape(HA, 2 * DHA, t)
    vt_ref[:, :2 * DHA, :] = vt.astype(BF16)
    vt_ref[:, 2 * DHA:, :] = jnp.ones((HA, BF16_SUBLANES, t), BF16)
    tb_ref[...] = lax.dot_general(wtb_ref[...], u, NT_DIMS, preferred_element_type=F32).astype(BF16)
    zn_ref[...] = jnp.dot(u, wn_ref[...], preferred_element_type=F32).astype(BF16)


def _proj(u, wqt, wvt, wtb, wn):
    t = ROW_TILE
    once = pl.Buffered(1)
    return pl.pallas_call(
        _proj_kernel,
        out_shape=(jax.ShapeDtypeStruct((HA, 2 * DHA, N_TILES * 2 * t), BF16),
                   jax.ShapeDtypeStruct((HA, N_TILES, VT_ROWS, t), BF16),
                   jax.ShapeDtypeStruct((TB_ROWS, P_PAD), BF16),
                   jax.ShapeDtypeStruct((P_PAD, NAT_W), BF16)),
        grid=(N_TILES,),
        in_specs=[pl.BlockSpec((t, D_MODEL), lambda i: (i, 0)),
                  pl.BlockSpec((QA_W, D_MODEL), lambda i: (0, 0), pipeline_mode=once),
                  pl.BlockSpec((QA_W, D_MODEL), lambda i: (0, 0), pipeline_mode=once),
                  pl.BlockSpec((TB_ROWS, D_MODEL), lambda i: (0, 0), pipeline_mode=once),
                  pl.BlockSpec((D_MODEL, NAT_W), lambda i: (0, 0), pipeline_mode=once)],
        out_specs=(pl.BlockSpec((HA, 2 * DHA, 2 * t), lambda i: (0, 0, i)),
                   pl.BlockSpec((HA, None, VT_ROWS, t), lambda i: (0, i, 0, 0)),
                   pl.BlockSpec((TB_ROWS, t), lambda i: (0, i)),
                   pl.BlockSpec((t, NAT_W), lambda i: (i, 0))),
        compiler_params=pltpu.CompilerParams(
            dimension_semantics=("parallel",), vmem_limit_bytes=56 * MIB),
        name="proj_mixers",
    )(u, wqt, wvt, wtb, wn)


def _merge_kernel(u_ref, ya_ref, ya_last_ref, yb_ref, h_ref, wg_ref, wa_ref, wb_ref, wo_ref,
                  gpost_ref, gnext_ref, h_out_ref, u_out_ref):
    u = u_ref[...]
    ya = jnp.where(pl.program_id(0) >= MAIN_ROWS // MERGE_TILE, ya_last_ref[...], ya_ref[...])
    gate_a = jax.nn.sigmoid(jnp.dot(u, wg_ref[:, :D_MODEL], preferred_element_type=F32))
    mix = gate_a * jnp.dot(ya, wa_ref[...], preferred_element_type=F32)
    gate_b = jax.nn.sigmoid(jnp.dot(u, wg_ref[:, D_MODEL:], preferred_element_type=F32))
    mix = mix + gate_b * jnp.dot(yb_ref[...], wb_ref[...], preferred_element_type=F32)
    t = jnp.dot(mix.astype(BF16), wo_ref[...], preferred_element_type=F32)
    hn = h_ref[...] + _rms(t, gpost_ref[...])
    h_out_ref[...] = hn
    u_out_ref[...] = _rms(hn, gnext_ref[...]).astype(BF16)


def _merge(u, ya, ya_last, yb, h, wg, wa, wb, wo, g_post, g_next):
    tm = MERGE_TILE
    once = pl.Buffered(1)
    main_tiles = MAIN_ROWS // tm
    return pl.pallas_call(
        _merge_kernel,
        out_shape=(jax.ShapeDtypeStruct((P_PAD, D_MODEL), F32),
                   jax.ShapeDtypeStruct((P_PAD, D_MODEL), BF16)),
        grid=(P_PAD // tm,),
        in_specs=[pl.BlockSpec((tm, D_MODEL), lambda i: (i, 0)),
                  pl.BlockSpec((tm, QA_W), lambda i: (jnp.minimum(i, main_tiles - 1), 0)),
                  pl.BlockSpec((tm, QA_W), lambda i: (jnp.maximum(i - main_tiles, 0), 0)),
                  pl.BlockSpec((tm, QB_W), lambda i: (i, 0)),
                  pl.BlockSpec((tm, D_MODEL), lambda i: (i, 0)),
                  pl.BlockSpec((D_MODEL, 2 * D_MODEL), lambda i: (0, 0), pipeline_mode=once),
                  pl.BlockSpec((QA_W, D_MODEL), lambda i: (0, 0), pipeline_mode=once),
                  pl.BlockSpec((QB_W, D_MODEL), lambda i: (0, 0), pipeline_mode=once),
                  pl.BlockSpec((D_MODEL, D_MODEL), lambda i: (0, 0), pipeline_mode=once),
                  pl.BlockSpec((1, D_MODEL), lambda i: (0, 0)),
                  pl.BlockSpec((1, D_MODEL), lambda i: (0, 0))],
        out_specs=(pl.BlockSpec((tm, D_MODEL), lambda i: (i, 0)),
                   pl.BlockSpec((tm, D_MODEL), lambda i: (i, 0))),
        compiler_params=pltpu.CompilerParams(
            dimension_semantics=("parallel",), vmem_limit_bytes=60 * MIB),
        name="gated_merge",
    )(u, ya, ya_last, yb, h, wg, wa, wb, wo, g_post.reshape(1, D_MODEL), g_next.reshape(1, D_MODEL))


def _ffn_kernel(u_ref, w1_ref, w2_ref, h_ref, gpost_ref, gnext_ref, h_out_ref, u_out_ref, acc_ref):
    f = pl.program_id(1)

    @pl.when(f == 0)
    def _():
        acc_ref[...] = jnp.zeros_like(acc_ref)

    a = jnp.dot(u_ref[...], w1_ref[...], preferred_element_type=F32)
    a = jnp.square(jnp.maximum(a, 0.0))
    acc_ref[...] += jnp.dot(a.astype(BF16), w2_ref[...], preferred_element_type=F32)

    @pl.when(f == pl.num_programs(1) - 1)
    def _():
        hn = h_ref[...] + _rms(acc_ref[...], gpost_ref[...])
        h_out_ref[...] = hn
        u_out_ref[...] = _rms(hn, gnext_ref[...]).astype(BF16)


def _ffn(u, w1, w2, h, g_post, g_next):
    tm, tf = ROW_TILE, FF_TILE
    return pl.pallas_call(
        _ffn_kernel,
        out_shape=(jax.ShapeDtypeStruct((P_PAD, D_MODEL), F32),
                   jax.ShapeDtypeStruct((P_PAD, D_MODEL), BF16)),
        grid=(N_TILES, D_FF // tf),
        in_specs=[pl.BlockSpec((tm, D_MODEL), lambda i, f: (i, 0)),
                  pl.BlockSpec((D_MODEL, tf), lambda i, f: (0, f)),
                  pl.BlockSpec((tf, D_MODEL), lambda i, f: (f, 0)),
                  pl.BlockSpec((tm, D_MODEL), lambda i, f: (i, 0)),
                  pl.BlockSpec((1, D_MODEL), lambda i, f: (0, 0)),
                  pl.BlockSpec((1, D_MODEL), lambda i, f: (0, 0))],
        out_specs=(pl.BlockSpec((tm, D_MODEL), lambda i, f: (i, 0)),
                   pl.BlockSpec((tm, D_MODEL), lambda i, f: (i, 0))),
        scratch_shapes=[pltpu.VMEM((tm, D_MODEL), F32)],
        compiler_params=pltpu.CompilerParams(
            dimension_semantics=("parallel", "arbitrary"), vmem_limit_bytes=56 * MIB),
        name="ffn_relu2",
    )(u, w1, w2, h, g_post.reshape(1, D_MODEL), g_next.reshape(1, D_MODEL))


def _attn_a_kernel(lamv_ref, subg_ref, qbd_ref, k_ref, kx_ref, vt_ref, bdiag_ref, bcorn_ref, bpad_ref,
                   o_ref, qx_ref, m_ref, acc_ref, *, lam_init, tq, first_tile):
    t = ROW_TILE
    i = pl.program_id(1) + first_tile
    j_below = jnp.maximum(i - 1, 0)
    n_far = jnp.maximum(i - 1, 0)
    q_op = qbd_ref[...] if tq == t else jnp.concatenate([qbd_ref[:, :tq], qbd_ref[:, t:t + tq]], axis=1)

    def square(b):
        return jnp.concatenate([b[:, :tq], b[:, :tq]], axis=1)

    def keys(j):
        return k_ref[pl.ds(pl.multiple_of(j * t, t), t), :]

    def emit(y):
        o_ref[:tq, :] = y.T.astype(o_ref.dtype)
        if tq < t:
            o_ref[tq:, :] = jnp.zeros((t - tq, 2 * DHA), o_ref.dtype)

    def finish(acc):
        lv = lamv_ref[...]
        lam = (jnp.exp(jnp.sum(lv[0:1] * lv[1:2], axis=-1, keepdims=True))
               - jnp.exp(jnp.sum(lv[2:3] * lv[3:4], axis=-1, keepdims=True)) + lam_init)
        o = acc[:2 * DHA] / acc[2 * DHA:2 * DHA + 1]
        od = o[:, :tq] - lam * o[:, tq:]
        y = od * lax.rsqrt(jnp.mean(od * od, axis=0, keepdims=True) + EPS) * subg_ref[...]
        return y * (1.0 - lam_init)

    s_diag = jnp.dot(keys(i), q_op, preferred_element_type=F32) + square(bdiag_ref[...])
    s_below = jnp.dot(keys(j_below), q_op, preferred_element_type=F32) + square(bcorn_ref[...])
    m = jnp.maximum(jnp.max(s_diag, axis=0, keepdims=True), jnp.max(s_below, axis=0, keepdims=True))
    offset_bf = (m + jnp.abs(m) * 2.0 ** -7).astype(BF16)
    offset = offset_bf.astype(F32)
    acc_ref[...] = (jnp.dot(vt_ref[i], jnp.exp2(s_diag - offset).astype(BF16), preferred_element_type=F32)
                    + jnp.dot(vt_ref[j_below], jnp.exp2(s_below - offset).astype(BF16),
                              preferred_element_type=F32))

    qx_ref[:2 * DHA, :] = q_op
    qx_ref[2 * DHA:2 * DHA + BF16_SUBLANES, :] = jnp.concatenate(
        [-offset_bf, jnp.full((1, 2 * tq), NEG, BF16), jnp.zeros((BF16_SUBLANES - 2, 2 * tq), BF16)], axis=0)
    qx_ref[2 * DHA + BF16_SUBLANES:, :] = jnp.zeros((QX_ROWS - 2 * DHA - BF16_SUBLANES, 2 * tq), BF16)

    def far_p(j):
        kx = jnp.concatenate([keys(j), kx_ref[jnp.minimum(j, 1)]], axis=1)
        return jnp.exp2(jnp.dot(kx, qx_ref[...], preferred_element_type=F32)).astype(BF16)

    def far_group(width):
        def body(g, first):
            ps = [far_p(first + w) for w in range(width)]
            pv = jnp.dot(vt_ref[first], ps[0], preferred_element_type=F32)
            for w in range(1, width):
                pv = pv + jnp.dot(vt_ref[first + w], ps[w], preferred_element_type=F32)
            acc_ref[...] += pv
            return first + width
        return body

    done = lax.fori_loop(0, n_far // FAR_GROUP, far_group(FAR_GROUP), jnp.int32(0))
    width = FAR_GROUP // 2
    while width >= 1:
        done = lax.fori_loop(0, (n_far % (2 * width)) // width, far_group(width), done)
        width //= 2

    y = finish(acc_ref[...])
    emit(y)
    all_finite = jnp.min(jnp.where(jnp.abs(y) < jnp.finfo(F32).max, 1.0, 0.0))

    @pl.when(all_finite < 0.5)
    def _():
        m_ref[...] = jnp.full(m_ref.shape, NEG, F32)
        acc_ref[...] = jnp.zeros_like(acc_ref)

        def tile(j, bias):
            s = jnp.dot(keys(j), q_op, preferred_element_type=F32)
            if bias is not None:
                s = s + bias
            m_prev = m_ref[...]
            m_new = jnp.maximum(m_prev, jnp.max(s, axis=0, keepdims=True))
            p = jnp.exp2(s - m_new).astype(BF16)
            acc_ref[...] = (jnp.exp2(m_prev - m_new) * acc_ref[...]
                            + jnp.dot(vt_ref[j], p, preferred_element_type=F32))
            m_ref[...] = m_new

        @pl.when(i >= 2)
        def _():
            tile(0, jnp.concatenate([bpad_ref[...]] * (2 * tq // LANES), axis=1))

        def plain(j, carry):
            tile(j, None)
            return carry

        lax.fori_loop(1, i - 1, plain, 0)

        @pl.when(i >= 1)
        def _():
            tile(i - 1, square(bcorn_ref[...]))

        tile(i, square(bdiag_ref[...]))
        emit(finish(acc_ref[...]))


def _attn_a(lamv, subg, qbd, zn, kx, vt, bdiag, bcorn, bpad, lam_init, first_tile, n_tiles, tq):
    t = ROW_TILE
    f = first_tile
    return pl.pallas_call(
        functools.partial(_attn_a_kernel, lam_init=lam_init, tq=tq, first_tile=f),
        out_shape=jax.ShapeDtypeStruct((n_tiles * t, QA_W), BF16),
        grid=(HA, n_tiles),
        in_specs=[pl.BlockSpec((4, DHA), lambda h, i: (0, 0)),
                  pl.BlockSpec((2 * DHA, 1), lambda h, i: (0, 0)),
                  pl.BlockSpec((None, 2 * DHA, 2 * t), lambda h, i: (h, 0, i + f)),
                  pl.BlockSpec((P_PAD, 2 * DHA), lambda h, i: (0, h)),
                  pl.BlockSpec((2, t, LANES), lambda h, i: (0, 0, 0)),
                  pl.BlockSpec((None, N_TILES, VT_ROWS, t), lambda h, i: (h, 0, 0, 0)),
                  pl.BlockSpec((None, None, t, t), lambda h, i: (h, jnp.minimum(i + f, 1), 0, 0)),
                  pl.BlockSpec((None, None, t, t), lambda h, i: (h, jnp.minimum(i + f, 2), 0, 0)),
                  pl.BlockSpec((t, LANES), lambda h, i: (0, 0))],
        out_specs=pl.BlockSpec((t, 2 * DHA), lambda h, i: (i, h)),
        scratch_shapes=[pltpu.VMEM((QX_ROWS, 2 * tq), BF16),
                        pltpu.VMEM((1, 2 * tq), F32),
                        pltpu.VMEM((VT_ROWS, 2 * tq), F32)],
        compiler_params=pltpu.CompilerParams(
            dimension_semantics=("parallel", "arbitrary"), vmem_limit_bytes=56 * MIB),
        name="diff_attention" if tq == t else "diff_attention_last",
    )(lamv, subg, qbd, zn, kx, vt, bdiag, bcorn, bpad)


def _attn_b_kernel(q_ref, km_ref, kp_ref, kc_ref, vm_ref, vp_ref, vc_ref, bias_ref, sink_ref, o_ref):
    pairs = G_B // 2
    first = lax.broadcasted_iota(jnp.int32, (1, LANES), 1) < DHB
    top = lax.broadcasted_iota(jnp.int32, (LANES, 1), 0) < DHB
    row = lax.broadcasted_iota(jnp.int32, (BF16_SUBLANES, 2 * BAND), 0)
    col = lax.broadcasted_iota(jnp.int32, (BF16_SUBLANES, 2 * BAND), 1)
    ones = jnp.where(row == col // BAND, 1.0, 0.0).astype(BF16)

    for g in range(HBKV):
        lanes = slice(LANES * g, LANES * (g + 1))
        qt = q_ref[PAIR_ROWS * g:PAIR_ROWS * (g + 1), :]
        rhs = jnp.concatenate([qt[LANES * p:LANES * (p + 1)] for p in range(pairs)], axis=1)
        kk = jnp.concatenate([km_ref[:, lanes], kp_ref[:, lanes], kc_ref[:, lanes]], axis=0)
        kk = jnp.concatenate([jnp.where(first, kk, jnp.zeros_like(kk)),
                              jnp.where(first, jnp.zeros_like(kk), kk)], axis=0)
        s = jnp.dot(kk, rhs, preferred_element_type=F32) + bias_ref[g]

        vt = jnp.concatenate([vm_ref[lanes, :], vp_ref[lanes, :], vc_ref[lanes, :]], axis=1)
        vt = jnp.concatenate([jnp.where(top, vt, jnp.zeros_like(vt)),
                              jnp.where(top, jnp.zeros_like(vt), vt)], axis=1)

        probs, maxes = [], []
        for e in range(2):
            se = s[e * BAND:(e + 1) * BAND]
            m = jnp.maximum(jnp.max(se, axis=0, keepdims=True), sink_ref[g, e])
            probs.append(jnp.exp(se - m).astype(BF16))
            maxes.append(m)
        acc = jnp.dot(jnp.concatenate([vt, ones], axis=0), jnp.concatenate(probs, axis=0),
                      preferred_element_type=F32)
        denom = [acc[2 * DHB + e:2 * DHB + e + 1] + jnp.exp(sink_ref[g, e] - maxes[e]) for e in range(2)]
        o = acc[:2 * DHB] / jnp.where(top, denom[0], denom[1])
        o_ref[:, PAIR_ROWS * g:PAIR_ROWS * (g + 1)] = jnp.concatenate(
            [o[:, BLK * p:BLK * (p + 1)].T for p in range(pairs)], axis=1).astype(o_ref.dtype)


def _attn_b(zn, tb, bias, sinks):
    kv_w = 2 * HBKV * DHB
    k_blk = QA_W // kv_w
    v_blk = QB_W // kv_w
    meta, prev, cur = (lambda n: 0), (lambda n: jnp.maximum(n - 1, 0)), (lambda n: n)

    def k_spec(row):
        return pl.BlockSpec((BLK, kv_w), lambda n: (row(n), k_blk))

    def v_spec(row):
        return pl.BlockSpec((kv_w, BLK), lambda n: (v_blk, row(n)))

    return pl.pallas_call(
        _attn_b_kernel,
        out_shape=jax.ShapeDtypeStruct((P_PAD, QB_W), BF16),
        grid=(N_BLOCKS,),
        in_specs=[pl.BlockSpec((QB_W, BLK), lambda n: (0, n)),
                  k_spec(meta), k_spec(prev), k_spec(cur),
                  v_spec(meta), v_spec(prev), v_spec(cur),
                  pl.BlockSpec((None, HBKV, 2 * BAND, PAIR_ROWS), lambda n: (jnp.minimum(n, 2), 0, 0, 0)),
                  pl.BlockSpec((HBKV, 2, 1, PAIR_ROWS), lambda n: (0, 0, 0, 0))],
        out_specs=pl.BlockSpec((BLK, QB_W), lambda n: (n, 0)),
        compiler_params=pltpu.CompilerParams(dimension_semantics=("parallel",)),
        name="swa_sink_attention",
    )(tb, zn, zn, zn, tb, tb, tb, bias, sinks)


def _t5_bucket(dist):
    n = jnp.maximum(dist, 0)
    max_exact = N_BUCKETS // 2
    nf = jnp.maximum(n, 1).astype(F32)
    large = max_exact + (jnp.log(nf / max_exact) / math.log(MAX_DIST / max_exact)
                         * (N_BUCKETS - max_exact)).astype(jnp.int32)
    large = jnp.minimum(large, N_BUCKETS - 1)
    return jnp.where(n < max_exact, n, large)


def _toeplitz(v, n):
    heads, span = v.shape
    flat = jnp.broadcast_to(v[:, None, :], (heads, n, span)).reshape(heads, n * span)
    return flat[:, :n * (span - 1)].reshape(heads, n, span - 1)[:, :, :n]


def _bias_tiles_a(rel_a):
    t = ROW_TILE
    span = 2 * t
    table = rel_a[_t5_bucket(jnp.arange(span))].astype(F32) - rel_a[N_BUCKETS - 1].astype(F32)
    table = table.T * LOG2E

    def toeplitz(v):
        return _toeplitz(v, t)

    kb = jnp.arange(t)[:, None]
    qa = jnp.arange(t)[None, :]
    pad = (kb < PAD_FRONT)[None]
    causal = jnp.where((qa >= kb)[None], toeplitz(table), NEG)
    corner = toeplitz(jnp.roll(table, -t, axis=1))
    diag = jnp.stack([jnp.where(pad, NEG, causal), causal], axis=1)
    corner = jnp.stack([jnp.full_like(corner, NEG), jnp.where(pad, NEG, corner), corner], axis=1)
    pad_col = jnp.broadcast_to(jnp.where(pad[0], NEG, 0.0).astype(F32), (t, LANES))
    lane = jnp.arange(LANES)[None, :]
    ones_col = (lane == 0)
    kx = jnp.stack([jnp.where(ones_col | ((lane == 1) & pad[0]), 1.0, 0.0),
                    jnp.broadcast_to(jnp.where(ones_col, 1.0, 0.0), (t, LANES))]).astype(BF16)
    return diag, corner, pad_col, kx


def _bias_tiles_b(rel_b):
    span = 4 * BLK
    table = rel_b[_t5_bucket(jnp.arange(span))].astype(F32).T
    kb = jnp.arange(BLK)[:, None]
    qa = jnp.arange(BLK)[None, :]
    is_meta = kb >= PAD_FRONT

    def segment(shift, ok):
        return jnp.where(ok[None], _toeplitz(jnp.roll(table, -shift, axis=1), BLK), NEG)

    prev = segment(BLK, kb > qa)
    cur = segment(0, qa >= kb)
    masked = jnp.full_like(cur, NEG)
    blocks = []
    for n in range(3):
        meta = segment(n * BLK, is_meta & (n * BLK + qa - kb >= 0))
        blocks.append(jnp.concatenate([meta, prev if n >= 2 else masked, cur if n >= 1 else masked], axis=1))
    per_head = jnp.stack(blocks, axis=0)
    pairs = G_B // 2
    stacked = per_head.reshape(3, HBKV, pairs, 2, BAND, BLK).transpose(0, 1, 3, 4, 2, 5)
    return stacked.reshape(3, HBKV, 2 * BAND, PAIR_ROWS)


def _sinks_b(sinks):
    pairs = G_B // 2
    s = sinks.astype(F32).reshape(HBKV, pairs, 2).transpose(0, 2, 1)
    return jnp.repeat(s, BLK, axis=2)[:, :, None, :]


def kernel(x, meta_tokens, rel_bias, w_in, w_branch_a, w_branch_b, w_out, lambda_q1, lambda_k1,
           lambda_q2, lambda_k2, diff_norm, sinks, norm_attn_pre, norm_attn_post, w_ff1, w_ff2,
           norm_ff_pre, norm_ff_post):
    assert x.shape == (1, SEQ, D_MODEL)
    h = jnp.concatenate([jnp.zeros((PAD_FRONT, D_MODEL), F32), meta_tokens.astype(F32), x[0],
                         jnp.zeros((P_PAD - P_STREAM, D_MODEL), F32)], axis=0)
    bdiag, bcorn, bpad, kx = _bias_tiles_a(rel_bias[:, :HA])
    bias_b = _bias_tiles_b(rel_bias[:, HA:HA + HBQ])

    u = _rmsnorm_rows(h, norm_attn_pre[0])
    for l in range(DEPTH):
        lam_init = 0.8 - 0.6 * math.exp(-0.3 * l)
        w = w_in[l]
        wqt = (w[:, :QA_W] * (DHA ** -0.5 * LOG2E)).T.astype(BF16)
        wvt = w[:, 2 * QA_W:3 * QA_W].T.astype(BF16)
        kv_b = w[:, 3 * QA_W + QB_W:GATE_OFF].reshape(D_MODEL, 2 * HBKV, 1, DHB)
        kv_b = jnp.broadcast_to(kv_b, (D_MODEL, 2 * HBKV, 2, DHB)).reshape(D_MODEL, 2, 2 * HBKV * DHB)
        wn = jnp.concatenate([w[:, QA_W:2 * QA_W], kv_b[:, 0]], axis=1).astype(BF16)
        wtb = jnp.concatenate([w[:, 3 * QA_W:3 * QA_W + QB_W] * DHB ** -0.5, kv_b[:, 1]], axis=1).T.astype(BF16)
        qbd, vt, tb, zn = _proj(u, wqt, wvt, wtb, wn)

        lamv = jnp.stack([lambda_q1[l], lambda_k1[l], lambda_q2[l], lambda_k2[l]]).astype(F32)
        a_args = (lamv, diff_norm[l].astype(F32).reshape(2 * DHA, 1), qbd, zn, kx, vt, bdiag, bcorn, bpad, lam_init)
        ya = _attn_a(*a_args, 0, N_TILES - 1, ROW_TILE)
        ya_last = _attn_a(*a_args, N_TILES - 1, 1, LAST_ROWS)
        yb = _attn_b(zn, tb, bias_b, _sinks_b(sinks[l]))

        h, u_ff = _merge(u, ya, ya_last, yb, h, w[:, GATE_OFF:].astype(BF16), w_branch_a[l].astype(BF16),
                         w_branch_b[l].astype(BF16), w_out[l].astype(BF16), norm_attn_post[l], norm_ff_pre[l])
        h, u = _ffn(u_ff, w_ff1[l].astype(BF16), w_ff2[l].astype(BF16), h, norm_ff_post[l],
                    norm_attn_pre[(l + 1) % DEPTH])
    return h[BLK:BLK + SEQ][None]
```

```python
import functools
import math

import jax
import jax.numpy as jnp
from jax import lax
from jax.experimental import pallas as pl
from jax.experimental.pallas import tpu as pltpu

D_MODEL = 2048
SEQ = 16384
DEPTH = 4
N_META = 16
BLK = 128
WINDOW = 128
HA = 8
DHA = 64
HBQ = 16
HBKV = 2
DHB = 64
G_B = HBQ // HBKV
D_FF = 4 * D_MODEL
N_BUCKETS = 32
MAX_DIST = 128
EPS = 1e-6
NEG = -1e30
LOG2E = math.log2(math.e)

QA_W = HA * 2 * DHA
QB_W = HBQ * DHB
GATE_OFF = 3 * QA_W + QB_W + 2 * HBKV * DHB
PAD_FRONT = BLK - N_META
P_STREAM = SEQ + BLK

LANES = 128
BF16_SUBLANES = 16
ROW_TILE = 512
P_PAD = -(-P_STREAM // ROW_TILE) * ROW_TILE
N_TILES = P_PAD // ROW_TILE
N_BLOCKS = P_PAD // BLK
MAIN_ROWS = (N_TILES - 1) * ROW_TILE
LAST_ROWS = P_STREAM - MAIN_ROWS
VT_ROWS = 2 * DHA + BF16_SUBLANES
QX_ROWS = 2 * LANES
PAIR_ROWS = (G_B // 2) * BLK
BAND = 3 * BLK
NAT_W = QA_W + 2 * HBKV * DHB
TB_ROWS = QB_W + 2 * HBKV * DHB
A_TILES_PER_STEP = 2
FAR_GROUP = 8
MERGE_TILE = 256
FF_TILE = 1024
MIB = 1024 * 1024

F32 = jnp.float32
BF16 = jnp.bfloat16
NT_DIMS = (((1,), (1,)), ((), ()))


def _rms(x, g):
    return x * lax.rsqrt(jnp.mean(x * x, axis=-1, keepdims=True) + EPS) * g


def _rmsnorm_kernel(h_ref, g_ref, o_ref):
    o_ref[...] = _rms(h_ref[...], g_ref[...]).astype(o_ref.dtype)


def _rmsnorm_rows(h, g):
    return pl.pallas_call(
        _rmsnorm_kernel,
        out_shape=jax.ShapeDtypeStruct(h.shape, BF16),
        grid=(N_TILES,),
        in_specs=[pl.BlockSpec((ROW_TILE, D_MODEL), lambda i: (i, 0)),
                  pl.BlockSpec((1, D_MODEL), lambda i: (0, 0))],
        out_specs=pl.BlockSpec((ROW_TILE, D_MODEL), lambda i: (i, 0)),
        compiler_params=pltpu.CompilerParams(dimension_semantics=("parallel",)),
        name="rmsnorm_rows",
    )(h, g.reshape(1, D_MODEL))


def _proj_kernel(u_ref, wqt_ref, wvt_ref, wtb_ref, wn_ref, qbd_ref, vt_ref, tb_ref, zn_ref):
    t = ROW_TILE
    u = u_ref[...]
    qt = lax.dot_general(wqt_ref[...], u, NT_DIMS, preferred_element_type=F32).reshape(HA, 2 * DHA, t)
    top = lax.broadcasted_iota(jnp.int32, (1, 2 * DHA, 1), 1) < DHA
    zero = jnp.zeros_like(qt)
    qbd_ref[...] = jnp.concatenate([jnp.where(top, qt, zero), jnp.where(top, zero, qt)], axis=2).astype(BF16)
    vt = lax.dot_general(wvt_ref[...], u, NT_DIMS, preferred_element_type=F32).reshape(HA, 2 * DHA, t)
    vt_ref[:, :2 * DHA, :] = vt.astype(BF16)
    vt_ref[:, 2 * DHA:, :] = jnp.ones((HA, BF16_SUBLANES, t), BF16)
    tb_ref[...] = lax.dot_general(wtb_ref[...], u, NT_DIMS, preferred_element_type=F32).astype(BF16)
    zn_ref[...] = jnp.dot(u, wn_ref[...], preferred_element_type=F32).astype(BF16)


def _proj(u, wqt, wvt, wtb, wn):
    t = ROW_TILE
    once = pl.Buffered(1)
    return pl.pallas_call(
        _proj_kernel,
        out_shape=(jax.ShapeDtypeStruct((HA, 2 * DHA, N_TILES * 2 * t), BF16),
                   jax.ShapeDtypeStruct((HA, N_TILES, VT_ROWS, t), BF16),
                   jax.ShapeDtypeStruct((TB_ROWS, P_PAD), BF16),
                   jax.ShapeDtypeStruct((P_PAD, NAT_W), BF16)),
        grid=(N_TILES,),
        in_specs=[pl.BlockSpec((t, D_MODEL), lambda i: (i, 0)),
                  pl.BlockSpec((QA_W, D_MODEL), lambda i: (0, 0), pipeline_mode=once),
                  pl.BlockSpec((QA_W, D_MODEL), lambda i: (0, 0), pipeline_mode=once),
                  pl.BlockSpec((TB_ROWS, D_MODEL), lambda i: (0, 0), pipeline_mode=once),
                  pl.BlockSpec((D_MODEL, NAT_W), lambda i: (0, 0), pipeline_mode=once)],
        out_specs=(pl.BlockSpec((HA, 2 * DHA, 2 * t), lambda i: (0, 0, i)),
                   pl.BlockSpec((HA, None, VT_ROWS, t), lambda i: (0, i, 0, 0)),
                   pl.BlockSpec((TB_ROWS, t), lambda i: (0, i)),
                   pl.BlockSpec((t, NAT_W), lambda i: (i, 0))),
        compiler_params=pltpu.CompilerParams(
            dimension_semantics=("parallel",), vmem_limit_bytes=56 * MIB),
        name="proj_mixers",
    )(u, wqt, wvt, wtb, wn)


def _merge_kernel(u_ref, ya_ref, ya_last_ref, yb_ref, h_ref, wg_ref, wa_ref, wb_ref, wo_ref,
                  gpost_ref, gnext_ref, h_out_ref, u_out_ref):
    u = u_ref[...]
    ya = jnp.where(pl.program_id(0) >= MAIN_ROWS // MERGE_TILE, ya_last_ref[...], ya_ref[...])
    gate_a = jax.nn.sigmoid(jnp.dot(u, wg_ref[:, :D_MODEL], preferred_element_type=F32))
    mix = gate_a * jnp.dot(ya, wa_ref[...], preferred_element_type=F32)
    gate_b = jax.nn.sigmoid(jnp.dot(u, wg_ref[:, D_MODEL:], preferred_element_type=F32))
    mix = mix + gate_b * jnp.dot(yb_ref[...], wb_ref[...], preferred_element_type=F32)
    t = jnp.dot(mix.astype(BF16), wo_ref[...], preferred_element_type=F32)
    hn = h_ref[...] + _rms(t, gpost_ref[...])
    h_out_ref[...] = hn
    u_out_ref[...] = _rms(hn, gnext_ref[...]).astype(BF16)


def _merge(u, ya, ya_last, yb, h, wg, wa, wb, wo, g_post, g_next):
    tm = MERGE_TILE
    once = pl.Buffered(1)
    main_tiles = MAIN_ROWS // tm
    return pl.pallas_call(
        _merge_kernel,
        out_shape=(jax.ShapeDtypeStruct((P_PAD, D_MODEL), F32),
                   jax.ShapeDtypeStruct((P_PAD, D_MODEL), BF16)),
        grid=(P_PAD // tm,),
        in_specs=[pl.BlockSpec((tm, D_MODEL), lambda i: (i, 0)),
                  pl.BlockSpec((tm, QA_W), lambda i: (jnp.minimum(i, main_tiles - 1), 0)),
                  pl.BlockSpec((tm, QA_W), lambda i: (jnp.maximum(i - main_tiles, 0), 0)),
                  pl.BlockSpec((tm, QB_W), lambda i: (i, 0)),
                  pl.BlockSpec((tm, D_MODEL), lambda i: (i, 0)),
                  pl.BlockSpec((D_MODEL, 2 * D_MODEL), lambda i: (0, 0), pipeline_mode=once),
                  pl.BlockSpec((QA_W, D_MODEL), lambda i: (0, 0), pipeline_mode=once),
                  pl.BlockSpec((QB_W, D_MODEL), lambda i: (0, 0), pipeline_mode=once),
                  pl.BlockSpec((D_MODEL, D_MODEL), lambda i: (0, 0), pipeline_mode=once),
                  pl.BlockSpec((1, D_MODEL), lambda i: (0, 0)),
                  pl.BlockSpec((1, D_MODEL), lambda i: (0, 0))],
        out_specs=(pl.BlockSpec((tm, D_MODEL), lambda i: (i, 0)),
                   pl.BlockSpec((tm, D_MODEL), lambda i: (i, 0))),
        compiler_params=pltpu.CompilerParams(
            dimension_semantics=("parallel",), vmem_limit_bytes=60 * MIB),
        name="gated_merge",
    )(u, ya, ya_last, yb, h, wg, wa, wb, wo, g_post.reshape(1, D_MODEL), g_next.reshape(1, D_MODEL))


def _ffn_kernel(u_ref, w1_ref, w2_ref, h_ref, gpost_ref, gnext_ref, h_out_ref, u_out_ref, acc_ref):
    f = pl.program_id(1)

    @pl.when(f == 0)
    def _():
        acc_ref[...] = jnp.zeros_like(acc_ref)

    a = jnp.dot(u_ref[...], w1_ref[...], preferred_element_type=F32)
    a = jnp.square(jnp.maximum(a, 0.0))
    acc_ref[...] += jnp.dot(a.astype(BF16), w2_ref[...], preferred_element_type=F32)

    @pl.when(f == pl.num_programs(1) - 1)
    def _():
        hn = h_ref[...] + _rms(acc_ref[...], gpost_ref[...])
        h_out_ref[...] = hn
        u_out_ref[...] = _rms(hn, gnext_ref[...]).astype(BF16)


def _ffn(u, w1, w2, h, g_post, g_next):
    tm, tf = ROW_TILE, FF_TILE
    return pl.pallas_call(
        _ffn_kernel,
        out_shape=(jax.ShapeDtypeStruct((P_PAD, D_MODEL), F32),
                   jax.ShapeDtypeStruct((P_PAD, D_MODEL), BF16)),
        grid=(N_TILES, D_FF // tf),
        in_specs=[pl.BlockSpec((tm, D_MODEL), lambda i, f: (i, 0)),
                  pl.BlockSpec((D_MODEL, tf), lambda i, f: (0, f)),
                  pl.BlockSpec((tf, D_MODEL), lambda i, f: (f, 0)),
                  pl.BlockSpec((tm, D_MODEL), lambda i, f: (i, 0)),
                  pl.BlockSpec((1, D_MODEL), lambda i, f: (0, 0)),
                  pl.BlockSpec((1, D_MODEL), lambda i, f: (0, 0))],
        out_specs=(pl.BlockSpec((tm, D_MODEL), lambda i, f: (i, 0)),
                   pl.BlockSpec((tm, D_MODEL), lambda i, f: (i, 0))),
        scratch_shapes=[pltpu.VMEM((tm, D_MODEL), F32)],
        compiler_params=pltpu.CompilerParams(
            dimension_semantics=("parallel", "arbitrary"), vmem_limit_bytes=56 * MIB),
        name="ffn_relu2",
    )(u, w1, w2, h, g_post.reshape(1, D_MODEL), g_next.reshape(1, D_MODEL))


def _attn_a_kernel(*refs, tiles_per_step, **static):
    for sub in range(tiles_per_step):
        _attn_a_tile(sub, *refs, tiles_per_step=tiles_per_step, **static)


def _attn_a_tile(sub, lamv_ref, subg_ref, qbd_ref, k_ref, kx_ref, vt_ref, bdiag_ref, bcorn_ref, bpad_ref,
                 o_ref, qx_ref, m_ref, acc_ref, *, lam_init, tq, first_tile, tiles_per_step):
    t = ROW_TILE
    i = pl.program_id(1) * tiles_per_step + sub + first_tile
    j_below = jnp.maximum(i - 1, 0)
    n_far = jnp.maximum(i - 1, 0)
    q0 = sub * 2 * t
    q_op = jnp.concatenate([qbd_ref[:, q0:q0 + tq], qbd_ref[:, q0 + t:q0 + t + tq]], axis=1)

    def square(b_ref, variant):
        b = b_ref[variant, :, :tq]
        return jnp.concatenate([b, b], axis=1)

    def bias_diag():
        return square(bdiag_ref, jnp.minimum(i, 1))

    def bias_below():
        return square(bcorn_ref, jnp.minimum(i, 2))

    def keys(j):
        return k_ref[pl.ds(pl.multiple_of(j * t, t), t), :]

    def emit(y):
        o_ref[sub * t:sub * t + tq, :] = y.T.astype(o_ref.dtype)
        if tq < t:
            o_ref[sub * t + tq:(sub + 1) * t, :] = jnp.zeros((t - tq, 2 * DHA), o_ref.dtype)

    def finish(acc):
        lv = lamv_ref[...]
        lam = (jnp.exp(jnp.sum(lv[0:1] * lv[1:2], axis=-1, keepdims=True))
               - jnp.exp(jnp.sum(lv[2:3] * lv[3:4], axis=-1, keepdims=True)) + lam_init)
        o = acc[:2 * DHA] / acc[2 * DHA:2 * DHA + 1]
        od = o[:, :tq] - lam * o[:, tq:]
        y = od * lax.rsqrt(jnp.mean(od * od, axis=0, keepdims=True) + EPS) * subg_ref[...]
        return y * (1.0 - lam_init)

    def extension(offset_row):
        return jnp.concatenate([offset_row, jnp.full((1, 2 * tq), NEG, BF16),
                                jnp.zeros((BF16_SUBLANES - 2, 2 * tq), BF16)], axis=0)

    qx_ref[:2 * DHA, :] = q_op
    qx_ref[2 * DHA:2 * DHA + BF16_SUBLANES, :] = extension(jnp.zeros((1, 2 * tq), BF16))
    qx_ref[2 * DHA + BF16_SUBLANES:, :] = jnp.zeros((QX_ROWS - 2 * DHA - BF16_SUBLANES, 2 * tq), BF16)

    def far_s(j, present=None):
        variant = jnp.minimum(j, 1) if present is None else jnp.where(present, jnp.minimum(j, 1), 2)
        kx = jnp.concatenate([keys(j), kx_ref[variant]], axis=1)
        return jnp.dot(kx, qx_ref[...], preferred_element_type=F32)

    s_diag = jnp.dot(keys(i), q_op, preferred_element_type=F32) + bias_diag()
    s_below = jnp.dot(keys(j_below), q_op, preferred_element_type=F32) + bias_below()
    j_last = jnp.maximum(n_far - 1, 0)
    j_prev = jnp.maximum(n_far - 2, 0)
    s_last = far_s(j_last, n_far >= 1)
    s_prev = far_s(j_prev, n_far >= 2)
    m = jnp.maximum(jnp.max(s_diag, axis=0, keepdims=True), jnp.max(s_below, axis=0, keepdims=True))
    offset_bf = (m + jnp.abs(m) * 2.0 ** -7).astype(BF16)
    offset = offset_bf.astype(F32)

    def pv(j, s):
        return jnp.dot(vt_ref[j], jnp.exp2(s - offset).astype(BF16), preferred_element_type=F32)

    acc_ref[...] = pv(i, s_diag) + pv(j_below, s_below) + pv(j_last, s_last) + pv(j_prev, s_prev)
    n_far = jnp.maximum(n_far - 2, 0)

    qx_ref[2 * DHA:2 * DHA + BF16_SUBLANES, :] = extension(-offset_bf)

    def far_p(j):
        return jnp.exp2(far_s(j)).astype(BF16)

    def far_group(width):
        def body(g, first):
            ps = [far_p(first + w) for w in range(width)]
            pv = jnp.dot(vt_ref[first], ps[0], preferred_element_type=F32)
            for w in range(1, width):
                pv = pv + jnp.dot(vt_ref[first + w], ps[w], preferred_element_type=F32)
            acc_ref[...] += pv
            return first + width
        return body

    done = lax.fori_loop(0, n_far // FAR_GROUP, far_group(FAR_GROUP), jnp.int32(0))
    width = FAR_GROUP // 2
    while width >= 1:
        done = lax.fori_loop(0, (n_far % (2 * width)) // width, far_group(width), done)
        width //= 2

    y = finish(acc_ref[...])
    emit(y)
    all_finite = jnp.min(jnp.where(jnp.abs(y) < jnp.finfo(F32).max, 1.0, 0.0))

    @pl.when(all_finite < 0.5)
    def _():
        m_ref[...] = jnp.full(m_ref.shape, NEG, F32)
        acc_ref[...] = jnp.zeros_like(acc_ref)

        def tile(j, bias):
            s = jnp.dot(keys(j), q_op, preferred_element_type=F32)
            if bias is not None:
                s = s + bias
            m_prev = m_ref[...]
            m_new = jnp.maximum(m_prev, jnp.max(s, axis=0, keepdims=True))
            p = jnp.exp2(s - m_new).astype(BF16)
            acc_ref[...] = (jnp.exp2(m_prev - m_new) * acc_ref[...]
                            + jnp.dot(vt_ref[j], p, preferred_element_type=F32))
            m_ref[...] = m_new

        @pl.when(i >= 2)
        def _():
            tile(0, jnp.concatenate([bpad_ref[...]] * (2 * tq // LANES), axis=1))

        def plain(j, carry):
            tile(j, None)
            return carry

        lax.fori_loop(1, i - 1, plain, 0)

        @pl.when(i >= 1)
        def _():
            tile(i - 1, bias_below())

        tile(i, bias_diag())
        emit(finish(acc_ref[...]))


def _attn_a(lamv, subg, qbd, zn, kx, vt, bdiag, bcorn, bpad, lam_init, first_tile, n_tiles, tq, tiles_per_step):
    t = ROW_TILE
    tps = tiles_per_step
    assert n_tiles % tps == 0 and first_tile % tps == 0
    f = first_tile // tps
    return pl.pallas_call(
        functools.partial(_attn_a_kernel, lam_init=lam_init, tq=tq, first_tile=first_tile, tiles_per_step=tps),
        out_shape=jax.ShapeDtypeStruct((n_tiles * t, QA_W), BF16),
        grid=(HA, n_tiles // tps),
        in_specs=[pl.BlockSpec((4, DHA), lambda h, i: (0, 0)),
                  pl.BlockSpec((2 * DHA, 1), lambda h, i: (0, 0)),
                  pl.BlockSpec((None, 2 * DHA, tps * 2 * t), lambda h, i: (h, 0, i + f)),
                  pl.BlockSpec((P_PAD, 2 * DHA), lambda h, i: (0, h)),
                  pl.BlockSpec((3, t, LANES), lambda h, i: (0, 0, 0)),
                  pl.BlockSpec((None, N_TILES, VT_ROWS, t), lambda h, i: (h, 0, 0, 0)),
                  pl.BlockSpec((None, 2, t, t), lambda h, i: (h, 0, 0, 0)),
                  pl.BlockSpec((None, 3, t, t), lambda h, i: (h, 0, 0, 0)),
                  pl.BlockSpec((t, LANES), lambda h, i: (0, 0))],
        out_specs=pl.BlockSpec((tps * t, 2 * DHA), lambda h, i: (i, h)),
        scratch_shapes=[pltpu.VMEM((QX_ROWS, 2 * tq), BF16),
                        pltpu.VMEM((1, 2 * tq), F32),
                        pltpu.VMEM((VT_ROWS, 2 * tq), F32)],
        compiler_params=pltpu.CompilerParams(
            dimension_semantics=("parallel", "arbitrary"), vmem_limit_bytes=56 * MIB),
        name="diff_attention" if tq == t else "diff_attention_last",
    )(lamv, subg, qbd, zn, kx, vt, bdiag, bcorn, bpad)


def _attn_b_kernel(q_ref, km_ref, kp_ref, kc_ref, vm_ref, vp_ref, vc_ref, bias_ref, sink_ref, o_ref):
    pairs = G_B // 2
    first = lax.broadcasted_iota(jnp.int32, (1, LANES), 1) < DHB
    top = lax.broadcasted_iota(jnp.int32, (LANES, 1), 0) < DHB
    row = lax.broadcasted_iota(jnp.int32, (BF16_SUBLANES, 2 * BAND), 0)
    col = lax.broadcasted_iota(jnp.int32, (BF16_SUBLANES, 2 * BAND), 1)
    ones = jnp.where(row == col // BAND, 1.0, 0.0).astype(BF16)

    for g in range(HBKV):
        lanes = slice(LANES * g, LANES * (g + 1))
        qt = q_ref[PAIR_ROWS * g:PAIR_ROWS * (g + 1), :]
        rhs = jnp.concatenate([qt[LANES * p:LANES * (p + 1)] for p in range(pairs)], axis=1)
        kk = jnp.concatenate([km_ref[:, lanes], kp_ref[:, lanes], kc_ref[:, lanes]], axis=0)
        kk = jnp.concatenate([jnp.where(first, kk, jnp.zeros_like(kk)),
                              jnp.where(first, jnp.zeros_like(kk), kk)], axis=0)
        s = jnp.dot(kk, rhs, preferred_element_type=F32) + bias_ref[g]

        vt = jnp.concatenate([vm_ref[lanes, :], vp_ref[lanes, :], vc_ref[lanes, :]], axis=1)
        vt = jnp.concatenate([jnp.where(top, vt, jnp.zeros_like(vt)),
                              jnp.where(top, jnp.zeros_like(vt), vt)], axis=1)

        probs, maxes = [], []
        for e in range(2):
            se = s[e * BAND:(e + 1) * BAND]
            m = jnp.maximum(jnp.max(se, axis=0, keepdims=True), sink_ref[g, e])
            probs.append(jnp.exp(se - m).astype(BF16))
            maxes.append(m)
        acc = jnp.dot(jnp.concatenate([vt, ones], axis=0), jnp.concatenate(probs, axis=0),
                      preferred_element_type=F32)
        denom = [acc[2 * DHB + e:2 * DHB + e + 1] + jnp.exp(sink_ref[g, e] - maxes[e]) for e in range(2)]
        o = acc[:2 * DHB] / jnp.where(top, denom[0], denom[1])
        o_ref[:, PAIR_ROWS * g:PAIR_ROWS * (g + 1)] = jnp.concatenate(
            [o[:, BLK * p:BLK * (p + 1)].T for p in range(pairs)], axis=1).astype(o_ref.dtype)


def _attn_b(zn, tb, bias, sinks):
    kv_w = 2 * HBKV * DHB
    k_blk = QA_W // kv_w
    v_blk = QB_W // kv_w
    meta, prev, cur = (lambda n: 0), (lambda n: jnp.maximum(n - 1, 0)), (lambda n: n)

    def k_spec(row):
        return pl.BlockSpec((BLK, kv_w), lambda n: (row(n), k_blk))

    def v_spec(row):
        return pl.BlockSpec((kv_w, BLK), lambda n: (v_blk, row(n)))

    return pl.pallas_call(
        _attn_b_kernel,
        out_shape=jax.ShapeDtypeStruct((P_PAD, QB_W), BF16),
        grid=(N_BLOCKS,),
        in_specs=[pl.BlockSpec((QB_W, BLK), lambda n: (0, n)),
                  k_spec(meta), k_spec(prev), k_spec(cur),
                  v_spec(meta), v_spec(prev), v_spec(cur),
                  pl.BlockSpec((None, HBKV, 2 * BAND, PAIR_ROWS), lambda n: (jnp.minimum(n, 2), 0, 0, 0)),
                  pl.BlockSpec((HBKV, 2, 1, PAIR_ROWS), lambda n: (0, 0, 0, 0))],
        out_specs=pl.BlockSpec((BLK, QB_W), lambda n: (n, 0)),
        compiler_params=pltpu.CompilerParams(dimension_semantics=("parallel",)),
        name="swa_sink_attention",
    )(tb, zn, zn, zn, tb, tb, tb, bias, sinks)


def _t5_bucket(dist):
    n = jnp.maximum(dist, 0)
    max_exact = N_BUCKETS // 2
    nf = jnp.maximum(n, 1).astype(F32)
    large = max_exact + (jnp.log(nf / max_exact) / math.log(MAX_DIST / max_exact)
                         * (N_BUCKETS - max_exact)).astype(jnp.int32)
    large = jnp.minimum(large, N_BUCKETS - 1)
    return jnp.where(n < max_exact, n, large)


def _toeplitz(v, n):
    heads, span = v.shape
    flat = jnp.broadcast_to(v[:, None, :], (heads, n, span)).reshape(heads, n * span)
    return flat[:, :n * (span - 1)].reshape(heads, n, span - 1)[:, :, :n]


def _bias_tiles_a(rel_a):
    t = ROW_TILE
    span = 2 * t
    table = rel_a[_t5_bucket(jnp.arange(span))].astype(F32) - rel_a[N_BUCKETS - 1].astype(F32)
    table = table.T * LOG2E

    def toeplitz(v):
        return _toeplitz(v, t)

    kb = jnp.arange(t)[:, None]
    qa = jnp.arange(t)[None, :]
    pad = (kb < PAD_FRONT)[None]
    causal = jnp.where((qa >= kb)[None], toeplitz(table), NEG)
    corner = toeplitz(jnp.roll(table, -t, axis=1))
    diag = jnp.stack([jnp.where(pad, NEG, causal), causal], axis=1)
    corner = jnp.stack([jnp.full_like(corner, NEG), jnp.where(pad, NEG, corner), corner], axis=1)
    pad_col = jnp.broadcast_to(jnp.where(pad[0], NEG, 0.0).astype(F32), (t, LANES))
    lane = jnp.arange(LANES)[None, :]
    ones_col = (lane == 0)
    flag_col = (lane == 1)
    kx = jnp.stack([jnp.where(ones_col | (flag_col & pad[0]), 1.0, 0.0),
                    jnp.broadcast_to(jnp.where(ones_col, 1.0, 0.0), (t, LANES)),
                    jnp.broadcast_to(jnp.where(ones_col | flag_col, 1.0, 0.0), (t, LANES))]).astype(BF16)
    return diag, corner, pad_col, kx


def _bias_tiles_b(rel_b):
    span = 4 * BLK
    table = rel_b[_t5_bucket(jnp.arange(span))].astype(F32).T
    kb = jnp.arange(BLK)[:, None]
    qa = jnp.arange(BLK)[None, :]
    is_meta = kb >= PAD_FRONT

    def segment(shift, ok):
        return jnp.where(ok[None], _toeplitz(jnp.roll(table, -shift, axis=1), BLK), NEG)

    prev = segment(BLK, kb > qa)
    cur = segment(0, qa >= kb)
    masked = jnp.full_like(cur, NEG)
    blocks = []
    for n in range(3):
        meta = segment(n * BLK, is_meta & (n * BLK + qa - kb >= 0))
        blocks.append(jnp.concatenate([meta, prev if n >= 2 else masked, cur if n >= 1 else masked], axis=1))
    per_head = jnp.stack(blocks, axis=0)
    pairs = G_B // 2
    stacked = per_head.reshape(3, HBKV, pairs, 2, BAND, BLK).transpose(0, 1, 3, 4, 2, 5)
    return stacked.reshape(3, HBKV, 2 * BAND, PAIR_ROWS)


def _sinks_b(sinks):
    pairs = G_B // 2
    s = sinks.astype(F32).reshape(HBKV, pairs, 2).transpose(0, 2, 1)
    return jnp.repeat(s, BLK, axis=2)[:, :, None, :]


def kernel(x, meta_tokens, rel_bias, w_in, w_branch_a, w_branch_b, w_out, lambda_q1, lambda_k1,
           lambda_q2, lambda_k2, diff_norm, sinks, norm_attn_pre, norm_attn_post, w_ff1, w_ff2,
           norm_ff_pre, norm_ff_post):
    assert x.shape == (1, SEQ, D_MODEL)
    h = jnp.concatenate([jnp.zeros((PAD_FRONT, D_MODEL), F32), meta_tokens.astype(F32), x[0],
                         jnp.zeros((P_PAD - P_STREAM, D_MODEL), F32)], axis=0)
    bdiag, bcorn, bpad, kx = _bias_tiles_a(rel_bias[:, :HA])
    bias_b = _bias_tiles_b(rel_bias[:, HA:HA + HBQ])

    u = _rmsnorm_rows(h, norm_attn_pre[0])
    for l in range(DEPTH):
        lam_init = 0.8 - 0.6 * math.exp(-0.3 * l)
        w = w_in[l]
        wqt = (w[:, :QA_W] * (DHA ** -0.5 * LOG2E)).T.astype(BF16)
        wvt = w[:, 2 * QA_W:3 * QA_W].T.astype(BF16)
        kv_b = w[:, 3 * QA_W + QB_W:GATE_OFF].reshape(D_MODEL, 2 * HBKV, 1, DHB)
        kv_b = jnp.broadcast_to(kv_b, (D_MODEL, 2 * HBKV, 2, DHB)).reshape(D_MODEL, 2, 2 * HBKV * DHB)
        wn = jnp.concatenate([w[:, QA_W:2 * QA_W], kv_b[:, 0]], axis=1).astype(BF16)
        wtb = jnp.concatenate([w[:, 3 * QA_W:3 * QA_W + QB_W] * DHB ** -0.5, kv_b[:, 1]], axis=1).T.astype(BF16)
        qbd, vt, tb, zn = _proj(u, wqt, wvt, wtb, wn)

        lamv = jnp.stack([lambda_q1[l], lambda_k1[l], lambda_q2[l], lambda_k2[l]]).astype(F32)
        a_args = (lamv, diff_norm[l].astype(F32).reshape(2 * DHA, 1), qbd, zn, kx, vt, bdiag, bcorn, bpad, lam_init)
        ya = _attn_a(*a_args, 0, N_TILES - 1, ROW_TILE, A_TILES_PER_STEP)
        ya_last = _attn_a(*a_args, N_TILES - 1, 1, LAST_ROWS, 1)
        yb = _attn_b(zn, tb, bias_b, _sinks_b(sinks[l]))

        h, u_ff = _merge(u, ya, ya_last, yb, h, w[:, GATE_OFF:].astype(BF16), w_branch_a[l].astype(BF16),
                         w_branch_b[l].astype(BF16), w_out[l].astype(BF16), norm_attn_post[l], norm_ff_pre[l])
        h, u = _ffn(u_ff, w_ff1[l].astype(BF16), w_ff2[l].astype(BF16), h, norm_ff_post[l],
                    norm_attn_pre[(l + 1) % DEPTH])
    return h[BLK:BLK + SEQ][None]
```

```python
import functools
import math

import jax
import jax.numpy as jnp
from jax import lax
from jax.experimental import pallas as pl
from jax.experimental.pallas import tpu as pltpu

D_MODEL = 2048
SEQ = 16384
DEPTH = 4
N_META = 16
BLK = 128
WINDOW = 128
HA = 8
DHA = 64
HBQ = 16
HBKV = 2
DHB = 64
G_B = HBQ // HBKV
D_FF = 4 * D_MODEL
N_BUCKETS = 32
MAX_DIST = 128
EPS = 1e-6
NEG = -1e30
LOG2E = math.log2(math.e)

QA_W = HA * 2 * DHA
QB_W = HBQ * DHB
GATE_OFF = 3 * QA_W + QB_W + 2 * HBKV * DHB
PAD_FRONT = BLK - N_META
P_STREAM = SEQ + BLK

LANES = 128
BF16_SUBLANES = 16
ROW_TILE = 512
P_PAD = -(-P_STREAM // ROW_TILE) * ROW_TILE
N_TILES = P_PAD // ROW_TILE
N_BLOCKS = P_PAD // BLK
MAIN_ROWS = (N_TILES - 1) * ROW_TILE
LAST_ROWS = P_STREAM - MAIN_ROWS
VT_ROWS = 2 * DHA + BF16_SUBLANES
QX_ROWS = 2 * LANES
PAIR_ROWS = (G_B // 2) * BLK
BAND = 3 * BLK
NAT_W = QA_W + 2 * HBKV * DHB
TB_ROWS = QB_W + 2 * HBKV * DHB
B_BLOCKS_PER_STEP = 4
A_TILES_PER_STEP = 1
FAR_GROUP = 8
MERGE_TILE = 256
FF_TILE = 1024
MIB = 1024 * 1024

F32 = jnp.float32
BF16 = jnp.bfloat16
NT_DIMS = (((1,), (1,)), ((), ()))


def _rms(x, g):
    return x * lax.rsqrt(jnp.mean(x * x, axis=-1, keepdims=True) + EPS) * g


def _rmsnorm_kernel(h_ref, g_ref, o_ref):
    o_ref[...] = _rms(h_ref[...], g_ref[...]).astype(o_ref.dtype)


def _rmsnorm_rows(h, g):
    return pl.pallas_call(
        _rmsnorm_kernel,
        out_shape=jax.ShapeDtypeStruct(h.shape, BF16),
        grid=(N_TILES,),
        in_specs=[pl.BlockSpec((ROW_TILE, D_MODEL), lambda i: (i, 0)),
                  pl.BlockSpec((1, D_MODEL), lambda i: (0, 0))],
        out_specs=pl.BlockSpec((ROW_TILE, D_MODEL), lambda i: (i, 0)),
        compiler_params=pltpu.CompilerParams(dimension_semantics=("parallel",)),
        name="rmsnorm_rows",
    )(h, g.reshape(1, D_MODEL))


def _proj_kernel(u_ref, wqt_ref, wvt_ref, wtb_ref, wn_ref, qbd_ref, vt_ref, tb_ref, zn_ref):
    t = ROW_TILE
    u = u_ref[...]
    qt = lax.dot_general(wqt_ref[...], u, NT_DIMS, preferred_element_type=F32).reshape(HA, 2 * DHA, t)
    top = lax.broadcasted_iota(jnp.int32, (1, 2 * DHA, 1), 1) < DHA
    zero = jnp.zeros_like(qt)
    qbd_ref[...] = jnp.concatenate([jnp.where(top, qt, zero), jnp.where(top, zero, qt)], axis=2).astype(BF16)
    vt = lax.dot_general(wvt_ref[...], u, NT_DIMS, preferred_element_type=F32).reshape(HA, 2 * DHA, t)
    vt_ref[:, :2 * DHA, :] = vt.astype(BF16)
    vt_ref[:, 2 * DHA:, :] = jnp.ones((HA, BF16_SUBLANES, t), BF16)
    tb_ref[...] = lax.dot_general(wtb_ref[...], u, NT_DIMS, preferred_element_type=F32).astype(BF16)
    zn_ref[...] = jnp.dot(u, wn_ref[...], preferred_element_type=F32).astype(BF16)


def _proj(u, wqt, wvt, wtb, wn):
    t = ROW_TILE
    once = pl.Buffered(1)
    return pl.pallas_call(
        _proj_kernel,
        out_shape=(jax.ShapeDtypeStruct((HA, 2 * DHA, N_TILES * 2 * t), BF16),
                   jax.ShapeDtypeStruct((HA, N_TILES, VT_ROWS, t), BF16),
                   jax.ShapeDtypeStruct((TB_ROWS, P_PAD), BF16),
                   jax.ShapeDtypeStruct((P_PAD, NAT_W), BF16)),
        grid=(N_TILES,),
        in_specs=[pl.BlockSpec((t, D_MODEL), lambda i: (i, 0)),
                  pl.BlockSpec((QA_W, D_MODEL), lambda i: (0, 0), pipeline_mode=once),
                  pl.BlockSpec((QA_W, D_MODEL), lambda i: (0, 0), pipeline_mode=once),
                  pl.BlockSpec((TB_ROWS, D_MODEL), lambda i: (0, 0), pipeline_mode=once),
                  pl.BlockSpec((D_MODEL, NAT_W), lambda i: (0, 0), pipeline_mode=once)],
        out_specs=(pl.BlockSpec((HA, 2 * DHA, 2 * t), lambda i: (0, 0, i)),
                   pl.BlockSpec((HA, None, VT_ROWS, t), lambda i: (0, i, 0, 0)),
                   pl.BlockSpec((TB_ROWS, t), lambda i: (0, i)),
                   pl.BlockSpec((t, NAT_W), lambda i: (i, 0))),
        compiler_params=pltpu.CompilerParams(
            dimension_semantics=("parallel",), vmem_limit_bytes=56 * MIB),
        name="proj_mixers",
    )(u, wqt, wvt, wtb, wn)


def _merge_kernel(u_ref, ya_ref, ya_last_ref, yb_ref, h_ref, wg_ref, wa_ref, wb_ref, wo_ref,
                  gpost_ref, gnext_ref, h_out_ref, u_out_ref):
    u = u_ref[...]
    ya = jnp.where(pl.program_id(0) >= MAIN_ROWS // MERGE_TILE, ya_last_ref[...], ya_ref[...])
    gate_a = jax.nn.sigmoid(jnp.dot(u, wg_ref[:, :D_MODEL], preferred_element_type=F32))
    mix = gate_a * jnp.dot(ya, wa_ref[...], preferred_element_type=F32)
    gate_b = jax.nn.sigmoid(jnp.dot(u, wg_ref[:, D_MODEL:], preferred_element_type=F32))
    mix = mix + gate_b * jnp.dot(yb_ref[...], wb_ref[...], preferred_element_type=F32)
    t = jnp.dot(mix.astype(BF16), wo_ref[...], preferred_element_type=F32)
    hn = h_ref[...] + _rms(t, gpost_ref[...])
    h_out_ref[...] = hn
    u_out_ref[...] = _rms(hn, gnext_ref[...]).astype(BF16)


def _merge(u, ya, ya_last, yb, h, wg, wa, wb, wo, g_post, g_next):
    tm = MERGE_TILE
    once = pl.Buffered(1)
    main_tiles = MAIN_ROWS // tm
    return pl.pallas_call(
        _merge_kernel,
        out_shape=(jax.ShapeDtypeStruct((P_PAD, D_MODEL), F32),
                   jax.ShapeDtypeStruct((P_PAD, D_MODEL), BF16)),
        grid=(P_PAD // tm,),
        in_specs=[pl.BlockSpec((tm, D_MODEL), lambda i: (i, 0)),
                  pl.BlockSpec((tm, QA_W), lambda i: (jnp.minimum(i, main_tiles - 1), 0)),
                  pl.BlockSpec((tm, QA_W), lambda i: (jnp.maximum(i - main_tiles, 0), 0)),
                  pl.BlockSpec((tm, QB_W), lambda i: (i, 0)),
                  pl.BlockSpec((tm, D_MODEL), lambda i: (i, 0)),
                  pl.BlockSpec((D_MODEL, 2 * D_MODEL), lambda i: (0, 0), pipeline_mode=once),
                  pl.BlockSpec((QA_W, D_MODEL), lambda i: (0, 0), pipeline_mode=once),
                  pl.BlockSpec((QB_W, D_MODEL), lambda i: (0, 0), pipeline_mode=once),
                  pl.BlockSpec((D_MODEL, D_MODEL), lambda i: (0, 0), pipeline_mode=once),
                  pl.BlockSpec((1, D_MODEL), lambda i: (0, 0)),
                  pl.BlockSpec((1, D_MODEL), lambda i: (0, 0))],
        out_specs=(pl.BlockSpec((tm, D_MODEL), lambda i: (i, 0)),
                   pl.BlockSpec((tm, D_MODEL), lambda i: (i, 0))),
        compiler_params=pltpu.CompilerParams(
            dimension_semantics=("parallel",), vmem_limit_bytes=60 * MIB),
        name="gated_merge",
    )(u, ya, ya_last, yb, h, wg, wa, wb, wo, g_post.reshape(1, D_MODEL), g_next.reshape(1, D_MODEL))


def _ffn_kernel(u_ref, w1_ref, w2_ref, h_ref, gpost_ref, gnext_ref, h_out_ref, u_out_ref, acc_ref):
    f = pl.program_id(1)

    @pl.when(f == 0)
    def _():
        acc_ref[...] = jnp.zeros_like(acc_ref)

    a = jnp.dot(u_ref[...], w1_ref[...], preferred_element_type=F32)
    a = jnp.square(jnp.maximum(a, 0.0))
    acc_ref[...] += jnp.dot(a.astype(BF16), w2_ref[...], preferred_element_type=F32)

    @pl.when(f == pl.num_programs(1) - 1)
    def _():
        hn = h_ref[...] + _rms(acc_ref[...], gpost_ref[...])
        h_out_ref[...] = hn
        u_out_ref[...] = _rms(hn, gnext_ref[...]).astype(BF16)


def _ffn(u, w1, w2, h, g_post, g_next):
    tm, tf = ROW_TILE, FF_TILE
    return pl.pallas_call(
        _ffn_kernel,
        out_shape=(jax.ShapeDtypeStruct((P_PAD, D_MODEL), F32),
                   jax.ShapeDtypeStruct((P_PAD, D_MODEL), BF16)),
        grid=(N_TILES, D_FF // tf),
        in_specs=[pl.BlockSpec((tm, D_MODEL), lambda i, f: (i, 0)),
                  pl.BlockSpec((D_MODEL, tf), lambda i, f: (0, f)),
                  pl.BlockSpec((tf, D_MODEL), lambda i, f: (f, 0)),
                  pl.BlockSpec((tm, D_MODEL), lambda i, f: (i, 0)),
                  pl.BlockSpec((1, D_MODEL), lambda i, f: (0, 0)),
                  pl.BlockSpec((1, D_MODEL), lambda i, f: (0, 0))],
        out_specs=(pl.BlockSpec((tm, D_MODEL), lambda i, f: (i, 0)),
                   pl.BlockSpec((tm, D_MODEL), lambda i, f: (i, 0))),
        scratch_shapes=[pltpu.VMEM((tm, D_MODEL), F32)],
        compiler_params=pltpu.CompilerParams(
            dimension_semantics=("parallel", "arbitrary"), vmem_limit_bytes=56 * MIB),
        name="ffn_relu2",
    )(u, w1, w2, h, g_post.reshape(1, D_MODEL), g_next.reshape(1, D_MODEL))


def _attn_a_kernel(*refs, tiles_per_step, **static):
    for sub in range(tiles_per_step):
        _attn_a_tile(sub, *refs, tiles_per_step=tiles_per_step, **static)


def _attn_a_tile(sub, lamv_ref, subg_ref, qbd_ref, k_ref, kx_ref, vt_ref, bdiag_ref, bcorn_ref, bpad_ref,
                 o_ref, qx_ref, m_ref, acc_ref, *, lam_init, tq, first_tile, tiles_per_step):
    t = ROW_TILE
    i = pl.program_id(1) * tiles_per_step + sub + first_tile
    j_below = jnp.maximum(i - 1, 0)
    n_far = jnp.maximum(i - 1, 0)
    q0 = sub * 2 * t
    q_op = jnp.concatenate([qbd_ref[:, q0:q0 + tq], qbd_ref[:, q0 + t:q0 + t + tq]], axis=1)

    def square(b_ref, variant):
        b = b_ref[variant, :, :tq]
        return jnp.concatenate([b, b], axis=1)

    def bias_diag():
        return square(bdiag_ref, jnp.minimum(i, 1))

    def bias_below():
        return square(bcorn_ref, jnp.minimum(i, 2))

    def keys(j):
        return k_ref[pl.ds(pl.multiple_of(j * t, t), t), :]

    def emit(y):
        o_ref[sub * t:sub * t + tq, :] = y.T.astype(o_ref.dtype)
        if tq < t:
            o_ref[sub * t + tq:(sub + 1) * t, :] = jnp.zeros((t - tq, 2 * DHA), o_ref.dtype)

    def finish(acc):
        lv = lamv_ref[...]
        lam = (jnp.exp(jnp.sum(lv[0:1] * lv[1:2], axis=-1, keepdims=True))
               - jnp.exp(jnp.sum(lv[2:3] * lv[3:4], axis=-1, keepdims=True)) + lam_init)
        o = acc[:2 * DHA] / acc[2 * DHA:2 * DHA + 1]
        od = o[:, :tq] - lam * o[:, tq:]
        y = od * lax.rsqrt(jnp.mean(od * od, axis=0, keepdims=True) + EPS) * subg_ref[...]
        return y * (1.0 - lam_init)

    s_diag = jnp.dot(keys(i), q_op, preferred_element_type=F32) + bias_diag()
    s_below = jnp.dot(keys(j_below), q_op, preferred_element_type=F32) + bias_below()
    m = jnp.maximum(jnp.max(s_diag, axis=0, keepdims=True), jnp.max(s_below, axis=0, keepdims=True))
    offset_bf = (m + jnp.abs(m) * 2.0 ** -7).astype(BF16)
    offset = offset_bf.astype(F32)
    acc_ref[...] = (jnp.dot(vt_ref[i], jnp.exp2(s_diag - offset).astype(BF16), preferred_element_type=F32)
                    + jnp.dot(vt_ref[j_below], jnp.exp2(s_below - offset).astype(BF16),
                              preferred_element_type=F32))

    qx_ref[:2 * DHA, :] = q_op
    qx_ref[2 * DHA:2 * DHA + BF16_SUBLANES, :] = jnp.concatenate(
        [-offset_bf, jnp.full((1, 2 * tq), NEG, BF16), jnp.zeros((BF16_SUBLANES - 2, 2 * tq), BF16)], axis=0)
    qx_ref[2 * DHA + BF16_SUBLANES:, :] = jnp.zeros((QX_ROWS - 2 * DHA - BF16_SUBLANES, 2 * tq), BF16)

    def far_p(j):
        kx = jnp.concatenate([keys(j), kx_ref[jnp.minimum(j, 1)]], axis=1)
        return jnp.exp2(jnp.dot(kx, qx_ref[...], preferred_element_type=F32)).astype(BF16)

    def far_group(width):
        def body(g, first):
            ps = [far_p(first + w) for w in range(width)]
            pv = jnp.dot(vt_ref[first], ps[0], preferred_element_type=F32)
            for w in range(1, width):
                pv = pv + jnp.dot(vt_ref[first + w], ps[w], preferred_element_type=F32)
            acc_ref[...] += pv
            return first + width
        return body

    done = lax.fori_loop(0, n_far // FAR_GROUP, far_group(FAR_GROUP), jnp.int32(0))
    width = FAR_GROUP // 2
    while width >= 1:
        done = lax.fori_loop(0, (n_far % (2 * width)) // width, far_group(width), done)
        width //= 2

    y = finish(acc_ref[...])
    emit(y)
    all_finite = jnp.min(jnp.where(jnp.abs(y) < jnp.finfo(F32).max, 1.0, 0.0))

    @pl.when(all_finite < 0.5)
    def _():
        m_ref[...] = jnp.full(m_ref.shape, NEG, F32)
        acc_ref[...] = jnp.zeros_like(acc_ref)

        def tile(j, bias):
            s = jnp.dot(keys(j), q_op, preferred_element_type=F32)
            if bias is not None:
                s = s + bias
            m_prev = m_ref[...]
            m_new = jnp.maximum(m_prev, jnp.max(s, axis=0, keepdims=True))
            p = jnp.exp2(s - m_new).astype(BF16)
            acc_ref[...] = (jnp.exp2(m_prev - m_new) * acc_ref[...]
                            + jnp.dot(vt_ref[j], p, preferred_element_type=F32))
            m_ref[...] = m_new

        @pl.when(i >= 2)
        def _():
            tile(0, jnp.concatenate([bpad_ref[...]] * (2 * tq // LANES), axis=1))

        def plain(j, carry):
            tile(j, None)
            return carry

        lax.fori_loop(1, i - 1, plain, 0)

        @pl.when(i >= 1)
        def _():
            tile(i - 1, bias_below())

        tile(i, bias_diag())
        emit(finish(acc_ref[...]))


def _attn_a(lamv, subg, qbd, zn, kx, vt, bdiag, bcorn, bpad, lam_init, first_tile, n_tiles, tq, tiles_per_step):
    t = ROW_TILE
    tps = tiles_per_step
    assert n_tiles % tps == 0 and first_tile % tps == 0
    f = first_tile // tps
    return pl.pallas_call(
        functools.partial(_attn_a_kernel, lam_init=lam_init, tq=tq, first_tile=first_tile, tiles_per_step=tps),
        out_shape=jax.ShapeDtypeStruct((n_tiles * t, QA_W), BF16),
        grid=(HA, n_tiles // tps),
        in_specs=[pl.BlockSpec((4, DHA), lambda h, i: (0, 0)),
                  pl.BlockSpec((2 * DHA, 1), lambda h, i: (0, 0)),
                  pl.BlockSpec((None, 2 * DHA, tps * 2 * t), lambda h, i: (h, 0, i + f)),
                  pl.BlockSpec((P_PAD, 2 * DHA), lambda h, i: (0, h)),
                  pl.BlockSpec((2, t, LANES), lambda h, i: (0, 0, 0)),
                  pl.BlockSpec((None, N_TILES, VT_ROWS, t), lambda h, i: (h, 0, 0, 0)),
                  pl.BlockSpec((None, 2, t, t), lambda h, i: (h, 0, 0, 0)),
                  pl.BlockSpec((None, 3, t, t), lambda h, i: (h, 0, 0, 0)),
                  pl.BlockSpec((t, LANES), lambda h, i: (0, 0))],
        out_specs=pl.BlockSpec((tps * t, 2 * DHA), lambda h, i: (i, h)),
        scratch_shapes=[pltpu.VMEM((QX_ROWS, 2 * tq), BF16),
                        pltpu.VMEM((1, 2 * tq), F32),
                        pltpu.VMEM((VT_ROWS, 2 * tq), F32)],
        compiler_params=pltpu.CompilerParams(
            dimension_semantics=("parallel", "arbitrary"), vmem_limit_bytes=56 * MIB),
        name="diff_attention" if tq == t else "diff_attention_last",
    )(lamv, subg, qbd, zn, kx, vt, bdiag, bcorn, bpad)


def _attn_b_kernel(q_ref, km_ref, kp_ref, kc_ref, vm_ref, vp_ref, vc_ref, bias_ref, sink_ref, o_ref):
    pairs = G_B // 2
    first = lax.broadcasted_iota(jnp.int32, (1, LANES), 1) < DHB
    top = lax.broadcasted_iota(jnp.int32, (LANES, 1), 0) < DHB
    row = lax.broadcasted_iota(jnp.int32, (BF16_SUBLANES, 2 * BAND), 0)
    col = lax.broadcasted_iota(jnp.int32, (BF16_SUBLANES, 2 * BAND), 1)
    ones = jnp.where(row == col // BAND, 1.0, 0.0).astype(BF16)

    for b in range(B_BLOCKS_PER_STEP):
        rows = slice(BLK * b, BLK * (b + 1))
        before = slice(BLK * (b - 1), BLK * b)
        variant = jnp.minimum(pl.program_id(0) * B_BLOCKS_PER_STEP + b, 2)
        for g in range(HBKV):
            lanes = slice(LANES * g, LANES * (g + 1))
            qt = q_ref[PAIR_ROWS * g:PAIR_ROWS * (g + 1), rows]
            rhs = jnp.concatenate([qt[LANES * p:LANES * (p + 1)] for p in range(pairs)], axis=1)
            k_prev = kp_ref[:, lanes] if b == 0 else kc_ref[before, lanes]
            kk = jnp.concatenate([km_ref[:, lanes], k_prev, kc_ref[rows, lanes]], axis=0)
            kk = jnp.concatenate([jnp.where(first, kk, jnp.zeros_like(kk)),
                                  jnp.where(first, jnp.zeros_like(kk), kk)], axis=0)
            s = jnp.dot(kk, rhs, preferred_element_type=F32) + bias_ref[variant, g]

            v_prev = vp_ref[lanes, :] if b == 0 else vc_ref[lanes, before]
            vt = jnp.concatenate([vm_ref[lanes, :], v_prev, vc_ref[lanes, rows]], axis=1)
            vt = jnp.concatenate([jnp.where(top, vt, jnp.zeros_like(vt)),
                                  jnp.where(top, jnp.zeros_like(vt), vt)], axis=1)

            probs, maxes = [], []
            for e in range(2):
                se = s[e * BAND:(e + 1) * BAND]
                m = jnp.maximum(jnp.max(se, axis=0, keepdims=True), sink_ref[g, e])
                probs.append(jnp.exp(se - m).astype(BF16))
                maxes.append(m)
            acc = jnp.dot(jnp.concatenate([vt, ones], axis=0), jnp.concatenate(probs, axis=0),
                          preferred_element_type=F32)
            denom = [acc[2 * DHB + e:2 * DHB + e + 1] + jnp.exp(sink_ref[g, e] - maxes[e]) for e in range(2)]
            o = acc[:2 * DHB] / jnp.where(top, denom[0], denom[1])
            o_ref[rows, PAIR_ROWS * g:PAIR_ROWS * (g + 1)] = jnp.concatenate(
                [o[:, BLK * p:BLK * (p + 1)].T for p in range(pairs)], axis=1).astype(o_ref.dtype)


def _attn_b(zn, tb, bias, sinks):
    nb = B_BLOCKS_PER_STEP
    kv_w = 2 * HBKV * DHB
    k_blk = QA_W // kv_w
    v_blk = QB_W // kv_w
    once = pl.Buffered(1)

    def before(s):
        return jnp.maximum(s * nb - 1, 0)

    return pl.pallas_call(
        _attn_b_kernel,
        out_shape=jax.ShapeDtypeStruct((P_PAD, QB_W), BF16),
        grid=(N_BLOCKS // nb,),
        in_specs=[pl.BlockSpec((QB_W, nb * BLK), lambda s: (0, s)),
                  pl.BlockSpec((BLK, kv_w), lambda s: (0, k_blk)),
                  pl.BlockSpec((BLK, kv_w), lambda s: (before(s), k_blk)),
                  pl.BlockSpec((nb * BLK, kv_w), lambda s: (s, k_blk)),
                  pl.BlockSpec((kv_w, BLK), lambda s: (v_blk, 0)),
                  pl.BlockSpec((kv_w, BLK), lambda s: (v_blk, before(s))),
                  pl.BlockSpec((kv_w, nb * BLK), lambda s: (v_blk, s)),
                  pl.BlockSpec((3, HBKV, 2 * BAND, PAIR_ROWS), lambda s: (0, 0, 0, 0), pipeline_mode=once),
                  pl.BlockSpec((HBKV, 2, 1, PAIR_ROWS), lambda s: (0, 0, 0, 0))],
        out_specs=pl.BlockSpec((nb * BLK, QB_W), lambda s: (s, 0)),
        compiler_params=pltpu.CompilerParams(dimension_semantics=("parallel",)),
        name="swa_sink_attention",
    )(tb, zn, zn, zn, tb, tb, tb, bias, sinks)


def _t5_bucket(dist):
    n = jnp.maximum(dist, 0)
    max_exact = N_BUCKETS // 2
    nf = jnp.maximum(n, 1).astype(F32)
    large = max_exact + (jnp.log(nf / max_exact) / math.log(MAX_DIST / max_exact)
                         * (N_BUCKETS - max_exact)).astype(jnp.int32)
    large = jnp.minimum(large, N_BUCKETS - 1)
    return jnp.where(n < max_exact, n, large)


def _toeplitz(v, n):
    heads, span = v.shape
    flat = jnp.broadcast_to(v[:, None, :], (heads, n, span)).reshape(heads, n * span)
    return flat[:, :n * (span - 1)].reshape(heads, n, span - 1)[:, :, :n]


def _bias_tiles_a(rel_a):
    t = ROW_TILE
    span = 2 * t
    table = rel_a[_t5_bucket(jnp.arange(span))].astype(F32) - rel_a[N_BUCKETS - 1].astype(F32)
    table = table.T * LOG2E

    def toeplitz(v):
        return _toeplitz(v, t)

    kb = jnp.arange(t)[:, None]
    qa = jnp.arange(t)[None, :]
    pad = (kb < PAD_FRONT)[None]
    causal = jnp.where((qa >= kb)[None], toeplitz(table), NEG)
    corner = toeplitz(jnp.roll(table, -t, axis=1))
    diag = jnp.stack([jnp.where(pad, NEG, causal), causal], axis=1)
    corner = jnp.stack([jnp.full_like(corner, NEG), jnp.where(pad, NEG, corner), corner], axis=1)
    pad_col = jnp.broadcast_to(jnp.where(pad[0], NEG, 0.0).astype(F32), (t, LANES))
    lane = jnp.arange(LANES)[None, :]
    ones_col = (lane == 0)
    kx = jnp.stack([jnp.where(ones_col | ((lane == 1) & pad[0]), 1.0, 0.0),
                    jnp.broadcast_to(jnp.where(ones_col, 1.0, 0.0), (t, LANES))]).astype(BF16)
    return diag, corner, pad_col, kx


def _bias_tiles_b(rel_b):
    span = 4 * BLK
    table = rel_b[_t5_bucket(jnp.arange(span))].astype(F32).T
    kb = jnp.arange(BLK)[:, None]
    qa = jnp.arange(BLK)[None, :]
    is_meta = kb >= PAD_FRONT

    def segment(shift, ok):
        return jnp.where(ok[None], _toeplitz(jnp.roll(table, -shift, axis=1), BLK), NEG)

    prev = segment(BLK, kb > qa)
    cur = segment(0, qa >= kb)
    masked = jnp.full_like(cur, NEG)
    blocks = []
    for n in range(3):
        meta = segment(n * BLK, is_meta & (n * BLK + qa - kb >= 0))
        blocks.append(jnp.concatenate([meta, prev if n >= 2 else masked, cur if n >= 1 else masked], axis=1))
    per_head = jnp.stack(blocks, axis=0)
    pairs = G_B // 2
    stacked = per_head.reshape(3, HBKV, pairs, 2, BAND, BLK).transpose(0, 1, 3, 4, 2, 5)
    return stacked.reshape(3, HBKV, 2 * BAND, PAIR_ROWS)


def _sinks_b(sinks):
    pairs = G_B // 2
    s = sinks.astype(F32).reshape(HBKV, pairs, 2).transpose(0, 2, 1)
    return jnp.repeat(s, BLK, axis=2)[:, :, None, :]


def kernel(x, meta_tokens, rel_bias, w_in, w_branch_a, w_branch_b, w_out, lambda_q1, lambda_k1,
           lambda_q2, lambda_k2, diff_norm, sinks, norm_attn_pre, norm_attn_post, w_ff1, w_ff2,
           norm_ff_pre, norm_ff_post):
    assert x.shape == (1, SEQ, D_MODEL)
    h = jnp.concatenate([jnp.zeros((PAD_FRONT, D_MODEL), F32), meta_tokens.astype(F32), x[0],
                         jnp.zeros((P_PAD - P_STREAM, D_MODEL), F32)], axis=0)
    bdiag, bcorn, bpad, kx = _bias_tiles_a(rel_bias[:, :HA])
    bias_b = _bias_tiles_b(rel_bias[:, HA:HA + HBQ])

    u = _rmsnorm_rows(h, norm_attn_pre[0])
    for l in range(DEPTH):
        lam_init = 0.8 - 0.6 * math.exp(-0.3 * l)
        w = w_in[l]
        wqt = (w[:, :QA_W] * (DHA ** -0.5 * LOG2E)).T.astype(BF16)
        wvt = w[:, 2 * QA_W:3 * QA_W].T.astype(BF16)
        kv_b = w[:, 3 * QA_W + QB_W:GATE_OFF].reshape(D_MODEL, 2 * HBKV, 1, DHB)
        kv_b = jnp.broadcast_to(kv_b, (D_MODEL, 2 * HBKV, 2, DHB)).reshape(D_MODEL, 2, 2 * HBKV * DHB)
        wn = jnp.concatenate([w[:, QA_W:2 * QA_W], kv_b[:, 0]], axis=1).astype(BF16)
        wtb = jnp.concatenate([w[:, 3 * QA_W:3 * QA_W + QB_W] * DHB ** -0.5, kv_b[:, 1]], axis=1).T.astype(BF16)
        qbd, vt, tb, zn = _proj(u, wqt, wvt, wtb, wn)

        lamv = jnp.stack([lambda_q1[l], lambda_k1[l], lambda_q2[l], lambda_k2[l]]).astype(F32)
        a_args = (lamv, diff_norm[l].astype(F32).reshape(2 * DHA, 1), qbd, zn, kx, vt, bdiag, bcorn, bpad, lam_init)
        ya = _attn_a(*a_args, 0, N_TILES - 1, ROW_TILE, A_TILES_PER_STEP)
        ya_last = _attn_a(*a_args, N_TILES - 1, 1, LAST_ROWS, 1)
        yb = _attn_b(zn, tb, bias_b, _sinks_b(sinks[l]))

        h, u_ff = _merge(u, ya, ya_last, yb, h, w[:, GATE_OFF:].astype(BF16), w_branch_a[l].astype(BF16),
                         w_branch_b[l].astype(BF16), w_out[l].astype(BF16), norm_attn_post[l], norm_ff_pre[l])
        h, u = _ffn(u_ff, w_ff1[l].astype(BF16), w_ff2[l].astype(BF16), h, norm_ff_post[l],
                    norm_attn_pre[(l + 1) % DEPTH])
    return h[BLK:BLK + SEQ][None]
```

```python
import functools
import math

import jax
import jax.numpy as jnp
from jax import lax
from jax.experimental import pallas as pl
from jax.experimental.pallas import tpu as pltpu

D_MODEL = 2048
SEQ = 16384
DEPTH = 4
N_META = 16
BLK = 128
WINDOW = 128
HA = 8
DHA = 64
HBQ = 16
HBKV = 2
DHB = 64
G_B = HBQ // HBKV
D_FF = 4 * D_MODEL
N_BUCKETS = 32
MAX_DIST = 128
EPS = 1e-6
NEG = -1e30
LOG2E = math.log2(math.e)

QA_W = HA * 2 * DHA
QB_W = HBQ * DHB
GATE_OFF = 3 * QA_W + QB_W + 2 * HBKV * DHB
PAD_FRONT = BLK - N_META
P_STREAM = SEQ + BLK

LANES = 128
BF16_SUBLANES = 16
ROW_TILE = 512
P_PAD = -(-P_STREAM // ROW_TILE) * ROW_TILE
N_TILES = P_PAD // ROW_TILE
N_BLOCKS = P_PAD // BLK
MAIN_ROWS = (N_TILES - 1) * ROW_TILE
LAST_ROWS = P_STREAM - MAIN_ROWS
VT_ROWS = 2 * DHA + BF16_SUBLANES
QX_ROWS = 2 * LANES
PAIR_ROWS = (G_B // 2) * BLK
BAND = 3 * BLK
NAT_W = QA_W + 2 * HBKV * DHB
TB_ROWS = QB_W + 2 * HBKV * DHB
B_BLOCKS_PER_STEP = 4
FAR_GROUP = 8
MERGE_TILE = 256
FF_TILE = 1024
MIB = 1024 * 1024

F32 = jnp.float32
BF16 = jnp.bfloat16
NT_DIMS = (((1,), (1,)), ((), ()))


def _rms(x, g):
    return x * lax.rsqrt(jnp.mean(x * x, axis=-1, keepdims=True) + EPS) * g


def _rmsnorm_kernel(h_ref, g_ref, o_ref):
    o_ref[...] = _rms(h_ref[...], g_ref[...]).astype(o_ref.dtype)


def _rmsnorm_rows(h, g):
    return pl.pallas_call(
        _rmsnorm_kernel,
        out_shape=jax.ShapeDtypeStruct(h.shape, BF16),
        grid=(N_TILES,),
        in_specs=[pl.BlockSpec((ROW_TILE, D_MODEL), lambda i: (i, 0)),
                  pl.BlockSpec((1, D_MODEL), lambda i: (0, 0))],
        out_specs=pl.BlockSpec((ROW_TILE, D_MODEL), lambda i: (i, 0)),
        compiler_params=pltpu.CompilerParams(dimension_semantics=("parallel",)),
        name="rmsnorm_rows",
    )(h, g.reshape(1, D_MODEL))


def _proj_kernel(u_ref, wqt_ref, wvt_ref, wtb_ref, wn_ref, qbd_ref, vt_ref, tb_ref, zn_ref):
    t = ROW_TILE
    u = u_ref[...]
    qt = lax.dot_general(wqt_ref[...], u, NT_DIMS, preferred_element_type=F32).reshape(HA, 2 * DHA, t)
    top = lax.broadcasted_iota(jnp.int32, (1, 2 * DHA, 1), 1) < DHA
    zero = jnp.zeros_like(qt)
    qbd_ref[...] = jnp.concatenate([jnp.where(top, qt, zero), jnp.where(top, zero, qt)], axis=2).astype(BF16)
    vt = lax.dot_general(wvt_ref[...], u, NT_DIMS, preferred_element_type=F32).reshape(HA, 2 * DHA, t)
    vt_ref[:, :2 * DHA, :] = vt.astype(BF16)
    vt_ref[:, 2 * DHA:, :] = jnp.ones((HA, BF16_SUBLANES, t), BF16)
    tb_ref[...] = lax.dot_general(wtb_ref[...], u, NT_DIMS, preferred_element_type=F32).astype(BF16)
    zn_ref[...] = jnp.dot(u, wn_ref[...], preferred_element_type=F32).astype(BF16)


def _proj(u, wqt, wvt, wtb, wn):
    t = ROW_TILE
    once = pl.Buffered(1)
    return pl.pallas_call(
        _proj_kernel,
        out_shape=(jax.ShapeDtypeStruct((HA, 2 * DHA, N_TILES * 2 * t), BF16),
                   jax.ShapeDtypeStruct((HA, N_TILES, VT_ROWS, t), BF16),
                   jax.ShapeDtypeStruct((TB_ROWS, P_PAD), BF16),
                   jax.ShapeDtypeStruct((P_PAD, NAT_W), BF16)),
        grid=(N_TILES,),
        in_specs=[pl.BlockSpec((t, D_MODEL), lambda i: (i, 0)),
                  pl.BlockSpec((QA_W, D_MODEL), lambda i: (0, 0), pipeline_mode=once),
                  pl.BlockSpec((QA_W, D_MODEL), lambda i: (0, 0), pipeline_mode=once),
                  pl.BlockSpec((TB_ROWS, D_MODEL), lambda i: (0, 0), pipeline_mode=once),
                  pl.BlockSpec((D_MODEL, NAT_W), lambda i: (0, 0), pipeline_mode=once)],
        out_specs=(pl.BlockSpec((HA, 2 * DHA, 2 * t), lambda i: (0, 0, i)),
                   pl.BlockSpec((HA, None, VT_ROWS, t), lambda i: (0, i, 0, 0)),
                   pl.BlockSpec((TB_ROWS, t), lambda i: (0, i)),
                   pl.BlockSpec((t, NAT_W), lambda i: (i, 0))),
        compiler_params=pltpu.CompilerParams(
            dimension_semantics=("parallel",), vmem_limit_bytes=56 * MIB),
        name="proj_mixers",
    )(u, wqt, wvt, wtb, wn)


def _merge_kernel(u_ref, ya_ref, ya_last_ref, yb_ref, h_ref, wg_ref, wa_ref, wb_ref, wo_ref,
                  gpost_ref, gnext_ref, h_out_ref, u_out_ref):
    u = u_ref[...]
    ya = jnp.where(pl.program_id(0) >= MAIN_ROWS // MERGE_TILE, ya_last_ref[...], ya_ref[...])
    gate_a = jax.nn.sigmoid(jnp.dot(u, wg_ref[:, :D_MODEL], preferred_element_type=F32))
    mix = gate_a * jnp.dot(ya, wa_ref[...], preferred_element_type=F32)
    gate_b = jax.nn.sigmoid(jnp.dot(u, wg_ref[:, D_MODEL:], preferred_element_type=F32))
    mix = mix + gate_b * jnp.dot(yb_ref[...], wb_ref[...], preferred_element_type=F32)
    t = jnp.dot(mix.astype(BF16), wo_ref[...], preferred_element_type=F32)
    hn = h_ref[...] + _rms(t, gpost_ref[...])
    h_out_ref[...] = hn
    u_out_ref[...] = _rms(hn, gnext_ref[...]).astype(BF16)


def _merge(u, ya, ya_last, yb, h, wg, wa, wb, wo, g_post, g_next):
    tm = MERGE_TILE
    once = pl.Buffered(1)
    main_tiles = MAIN_ROWS // tm
    return pl.pallas_call(
        _merge_kernel,
        out_shape=(jax.ShapeDtypeStruct((P_PAD, D_MODEL), F32),
                   jax.ShapeDtypeStruct((P_PAD, D_MODEL), BF16)),
        grid=(P_PAD // tm,),
        in_specs=[pl.BlockSpec((tm, D_MODEL), lambda i: (i, 0)),
                  pl.BlockSpec((tm, QA_W), lambda i: (jnp.minimum(i, main_tiles - 1), 0)),
                  pl.BlockSpec((tm, QA_W), lambda i: (jnp.maximum(i - main_tiles, 0), 0)),
                  pl.BlockSpec((tm, QB_W), lambda i: (i, 0)),
                  pl.BlockSpec((tm, D_MODEL), lambda i: (i, 0)),
                  pl.BlockSpec((D_MODEL, 2 * D_MODEL), lambda i: (0, 0), pipeline_mode=once),
                  pl.BlockSpec((QA_W, D_MODEL), lambda i: (0, 0), pipeline_mode=once),
                  pl.BlockSpec((QB_W, D_MODEL), lambda i: (0, 0), pipeline_mode=once),
                  pl.BlockSpec((D_MODEL, D_MODEL), lambda i: (0, 0), pipeline_mode=once),
                  pl.BlockSpec((1, D_MODEL), lambda i: (0, 0)),
                  pl.BlockSpec((1, D_MODEL), lambda i: (0, 0))],
        out_specs=(pl.BlockSpec((tm, D_MODEL), lambda i: (i, 0)),
                   pl.BlockSpec((tm, D_MODEL), lambda i: (i, 0))),
        compiler_params=pltpu.CompilerParams(
            dimension_semantics=("parallel",), vmem_limit_bytes=60 * MIB),
        name="gated_merge",
    )(u, ya, ya_last, yb, h, wg, wa, wb, wo, g_post.reshape(1, D_MODEL), g_next.reshape(1, D_MODEL))


def _ffn_kernel(u_ref, w1_ref, w2_ref, h_ref, gpost_ref, gnext_ref, h_out_ref, u_out_ref, acc_ref):
    f = pl.program_id(1)

    @pl.when(f == 0)
    def _():
        acc_ref[...] = jnp.zeros_like(acc_ref)

    a = jnp.dot(u_ref[...], w1_ref[...], preferred_element_type=F32)
    a = jnp.square(jnp.maximum(a, 0.0))
    acc_ref[...] += jnp.dot(a.astype(BF16), w2_ref[...], preferred_element_type=F32)

    @pl.when(f == pl.num_programs(1) - 1)
    def _():
        hn = h_ref[...] + _rms(acc_ref[...], gpost_ref[...])
        h_out_ref[...] = hn
        u_out_ref[...] = _rms(hn, gnext_ref[...]).astype(BF16)


def _ffn(u, w1, w2, h, g_post, g_next):
    tm, tf = ROW_TILE, FF_TILE
    return pl.pallas_call(
        _ffn_kernel,
        out_shape=(jax.ShapeDtypeStruct((P_PAD, D_MODEL), F32),
                   jax.ShapeDtypeStruct((P_PAD, D_MODEL), BF16)),
        grid=(N_TILES, D_FF // tf),
        in_specs=[pl.BlockSpec((tm, D_MODEL), lambda i, f: (i, 0)),
                  pl.BlockSpec((D_MODEL, tf), lambda i, f: (0, f)),
                  pl.BlockSpec((tf, D_MODEL), lambda i, f: (f, 0)),
                  pl.BlockSpec((tm, D_MODEL), lambda i, f: (i, 0)),
                  pl.BlockSpec((1, D_MODEL), lambda i, f: (0, 0)),
                  pl.BlockSpec((1, D_MODEL), lambda i, f: (0, 0))],
        out_specs=(pl.BlockSpec((tm, D_MODEL), lambda i, f: (i, 0)),
                   pl.BlockSpec((tm, D_MODEL), lambda i, f: (i, 0))),
        scratch_shapes=[pltpu.VMEM((tm, D_MODEL), F32)],
        compiler_params=pltpu.CompilerParams(
            dimension_semantics=("parallel", "arbitrary"), vmem_limit_bytes=56 * MIB),
        name="ffn_relu2",
    )(u, w1, w2, h, g_post.reshape(1, D_MODEL), g_next.reshape(1, D_MODEL))


def _attn_a_kernel(lamv_ref, subg_ref, qbd_ref, k_ref, kx_ref, vt_ref, bdiag_ref, bcorn_ref, bpad_ref,
                   o_ref, qx_ref, m_ref, acc_ref, *, lam_init, tq, first_tile):
    t = ROW_TILE
    i = pl.program_id(1) + first_tile
    j_below = jnp.maximum(i - 1, 0)
    n_far = jnp.maximum(i - 1, 0)
    q_op = qbd_ref[...] if tq == t else jnp.concatenate([qbd_ref[:, :tq], qbd_ref[:, t:t + tq]], axis=1)

    def square(b_ref):
        b = b_ref[:, :tq]
        return jnp.concatenate([b, b], axis=1)

    def bias_diag():
        return square(bdiag_ref)

    def bias_below():
        return square(bcorn_ref)

    def keys(j):
        return k_ref[pl.ds(pl.multiple_of(j * t, t), t), :]

    def emit(y):
        o_ref[:tq, :] = y.T.astype(o_ref.dtype)
        if tq < t:
            o_ref[tq:, :] = jnp.zeros((t - tq, 2 * DHA), o_ref.dtype)

    def finish(acc):
        lv = lamv_ref[...]
        lam = (jnp.exp(jnp.sum(lv[0:1] * lv[1:2], axis=-1, keepdims=True))
               - jnp.exp(jnp.sum(lv[2:3] * lv[3:4], axis=-1, keepdims=True)) + lam_init)
        o = acc[:2 * DHA] / acc[2 * DHA:2 * DHA + 1]
        od = o[:, :tq] - lam * o[:, tq:]
        y = od * lax.rsqrt(jnp.mean(od * od, axis=0, keepdims=True) + EPS) * subg_ref[...]
        return y * (1.0 - lam_init)

    s_diag = jnp.dot(keys(i), q_op, preferred_element_type=F32) + bias_diag()
    s_below = jnp.dot(keys(j_below), q_op, preferred_element_type=F32) + bias_below()
    m = jnp.maximum(jnp.max(s_diag, axis=0, keepdims=True), jnp.max(s_below, axis=0, keepdims=True))
    offset_bf = (m + jnp.abs(m) * 2.0 ** -7).astype(BF16)
    offset = offset_bf.astype(F32)
    acc_ref[...] = (jnp.dot(vt_ref[i], jnp.exp2(s_diag - offset).astype(BF16), preferred_element_type=F32)
                    + jnp.dot(vt_ref[j_below], jnp.exp2(s_below - offset).astype(BF16),
                              preferred_element_type=F32))

    qx_ref[:2 * DHA, :] = q_op
    qx_ref[2 * DHA:2 * DHA + BF16_SUBLANES, :] = jnp.concatenate(
        [-offset_bf, jnp.full((1, 2 * tq), NEG, BF16), jnp.zeros((BF16_SUBLANES - 2, 2 * tq), BF16)], axis=0)
    qx_ref[2 * DHA + BF16_SUBLANES:, :] = jnp.zeros((QX_ROWS - 2 * DHA - BF16_SUBLANES, 2 * tq), BF16)

    def far_p(j):
        kx = jnp.concatenate([keys(j), kx_ref[jnp.minimum(j, 1)]], axis=1)
        return jnp.exp2(jnp.dot(kx, qx_ref[...], preferred_element_type=F32)).astype(BF16)

    def far_group(width):
        def body(g, first):
            ps = [far_p(first + w) for w in range(width)]
            pv = jnp.dot(vt_ref[first], ps[0], preferred_element_type=F32)
            for w in range(1, width):
                pv = pv + jnp.dot(vt_ref[first + w], ps[w], preferred_element_type=F32)
            acc_ref[...] += pv
            return first + width
        return body

    done = lax.fori_loop(0, n_far // FAR_GROUP, far_group(FAR_GROUP), jnp.int32(0))
    width = FAR_GROUP // 2
    while width >= 1:
        done = lax.fori_loop(0, (n_far % (2 * width)) // width, far_group(width), done)
        width //= 2

    y = finish(acc_ref[...])
    emit(y)
    all_finite = jnp.min(jnp.where(jnp.abs(y) < jnp.finfo(F32).max, 1.0, 0.0))

    @pl.when(all_finite < 0.5)
    def _():
        m_ref[...] = jnp.full(m_ref.shape, NEG, F32)
        acc_ref[...] = jnp.zeros_like(acc_ref)

        def tile(j, bias):
            s = jnp.dot(keys(j), q_op, preferred_element_type=F32)
            if bias is not None:
                s = s + bias
            m_prev = m_ref[...]
            m_new = jnp.maximum(m_prev, jnp.max(s, axis=0, keepdims=True))
            p = jnp.exp2(s - m_new).astype(BF16)
            acc_ref[...] = (jnp.exp2(m_prev - m_new) * acc_ref[...]
                            + jnp.dot(vt_ref[j], p, preferred_element_type=F32))
            m_ref[...] = m_new

        @pl.when(i >= 2)
        def _():
            tile(0, jnp.concatenate([bpad_ref[...]] * (2 * tq // LANES), axis=1))

        def plain(j, carry):
            tile(j, None)
            return carry

        lax.fori_loop(1, i - 1, plain, 0)

        @pl.when(i >= 1)
        def _():
            tile(i - 1, bias_below())

        tile(i, bias_diag())
        emit(finish(acc_ref[...]))


def _attn_a(lamv, subg, qbd, zn, kx, vt, bdiag, bcorn, bpad, lam_init, first_tile, n_tiles, tq):
    t = ROW_TILE
    f = first_tile
    return pl.pallas_call(
        functools.partial(_attn_a_kernel, lam_init=lam_init, tq=tq, first_tile=f),
        out_shape=jax.ShapeDtypeStruct((n_tiles * t, QA_W), BF16),
        grid=(HA, n_tiles),
        in_specs=[pl.BlockSpec((4, DHA), lambda h, i: (0, 0)),
                  pl.BlockSpec((2 * DHA, 1), lambda h, i: (0, 0)),
                  pl.BlockSpec((None, 2 * DHA, 2 * t), lambda h, i: (h, 0, i + f)),
                  pl.BlockSpec((P_PAD, 2 * DHA), lambda h, i: (0, h)),
                  pl.BlockSpec((2, t, LANES), lambda h, i: (0, 0, 0)),
                  pl.BlockSpec((None, N_TILES, VT_ROWS, t), lambda h, i: (h, 0, 0, 0)),
                  pl.BlockSpec((None, None, t, t), lambda h, i: (h, jnp.minimum(i + f, 1), 0, 0)),
                  pl.BlockSpec((None, None, t, t), lambda h, i: (h, jnp.minimum(i + f, 2), 0, 0)),
                  pl.BlockSpec((t, LANES), lambda h, i: (0, 0))],
        out_specs=pl.BlockSpec((t, 2 * DHA), lambda h, i: (i, h)),
        scratch_shapes=[pltpu.VMEM((QX_ROWS, 2 * tq), BF16),
                        pltpu.VMEM((1, 2 * tq), F32),
                        pltpu.VMEM((VT_ROWS, 2 * tq), F32)],
        compiler_params=pltpu.CompilerParams(
            dimension_semantics=("parallel", "arbitrary"), vmem_limit_bytes=56 * MIB),
        name="diff_attention" if tq == t else "diff_attention_last",
    )(lamv, subg, qbd, zn, kx, vt, bdiag, bcorn, bpad)


def _attn_b_kernel(q_ref, km_ref, kp_ref, kc_ref, vm_ref, vp_ref, vc_ref, bias_ref, sink_ref, o_ref):
    pairs = G_B // 2
    first = lax.broadcasted_iota(jnp.int32, (1, LANES), 1) < DHB
    top = lax.broadcasted_iota(jnp.int32, (LANES, 1), 0) < DHB
    row = lax.broadcasted_iota(jnp.int32, (BF16_SUBLANES, 2 * BAND), 0)
    col = lax.broadcasted_iota(jnp.int32, (BF16_SUBLANES, 2 * BAND), 1)
    ones = jnp.where(row == col // BAND, 1.0, 0.0).astype(BF16)

    for b in range(B_BLOCKS_PER_STEP):
        rows = slice(BLK * b, BLK * (b + 1))
        before = slice(BLK * (b - 1), BLK * b)
        variant = jnp.minimum(pl.program_id(0) * B_BLOCKS_PER_STEP + b, 2)
        for g in range(HBKV):
            lanes = slice(LANES * g, LANES * (g + 1))
            qt = q_ref[PAIR_ROWS * g:PAIR_ROWS * (g + 1), rows]
            rhs = jnp.concatenate([qt[LANES * p:LANES * (p + 1)] for p in range(pairs)], axis=1)
            k_prev = kp_ref[:, lanes] if b == 0 else kc_ref[before, lanes]
            kk = jnp.concatenate([km_ref[:, lanes], k_prev, kc_ref[rows, lanes]], axis=0)
            kk = jnp.concatenate([jnp.where(first, kk, jnp.zeros_like(kk)),
                                  jnp.where(first, jnp.zeros_like(kk), kk)], axis=0)
            s = jnp.dot(kk, rhs, preferred_element_type=F32) + bias_ref[variant, g]

            v_prev = vp_ref[lanes, :] if b == 0 else vc_ref[lanes, before]
            vt = jnp.concatenate([vm_ref[lanes, :], v_prev, vc_ref[lanes, rows]], axis=1)
            vt = jnp.concatenate([jnp.where(top, vt, jnp.zeros_like(vt)),
                                  jnp.where(top, jnp.zeros_like(vt), vt)], axis=1)

            probs, maxes = [], []
            for e in range(2):
                se = s[e * BAND:(e + 1) * BAND]
                m = jnp.maximum(jnp.max(se, axis=0, keepdims=True), sink_ref[g, e])
                probs.append(jnp.exp(se - m).astype(BF16))
                maxes.append(m)
            acc = jnp.dot(jnp.concatenate([vt, ones], axis=0), jnp.concatenate(probs, axis=0),
                          preferred_element_type=F32)
            denom = [acc[2 * DHB + e:2 * DHB + e + 1] + jnp.exp(sink_ref[g, e] - maxes[e]) for e in range(2)]
            o = acc[:2 * DHB] / jnp.where(top, denom[0], denom[1])
            o_ref[rows, PAIR_ROWS * g:PAIR_ROWS * (g + 1)] = jnp.concatenate(
                [o[:, BLK * p:BLK * (p + 1)].T for p in range(pairs)], axis=1).astype(o_ref.dtype)


def _attn_b(zn, tb, bias, sinks):
    nb = B_BLOCKS_PER_STEP
    kv_w = 2 * HBKV * DHB
    k_blk = QA_W // kv_w
    v_blk = QB_W // kv_w
    once = pl.Buffered(1)

    def before(s):
        return jnp.maximum(s * nb - 1, 0)

    return pl.pallas_call(
        _attn_b_kernel,
        out_shape=jax.ShapeDtypeStruct((P_PAD, QB_W), BF16),
        grid=(N_BLOCKS // nb,),
        in_specs=[pl.BlockSpec((QB_W, nb * BLK), lambda s: (0, s)),
                  pl.BlockSpec((BLK, kv_w), lambda s: (0, k_blk)),
                  pl.BlockSpec((BLK, kv_w), lambda s: (before(s), k_blk)),
                  pl.BlockSpec((nb * BLK, kv_w), lambda s: (s, k_blk)),
                  pl.BlockSpec((kv_w, BLK), lambda s: (v_blk, 0)),
                  pl.BlockSpec((kv_w, BLK), lambda s: (v_blk, before(s))),
                  pl.BlockSpec((kv_w, nb * BLK), lambda s: (v_blk, s)),
                  pl.BlockSpec((3, HBKV, 2 * BAND, PAIR_ROWS), lambda s: (0, 0, 0, 0), pipeline_mode=once),
                  pl.BlockSpec((HBKV, 2, 1, PAIR_ROWS), lambda s: (0, 0, 0, 0))],
        out_specs=pl.BlockSpec((nb * BLK, QB_W), lambda s: (s, 0)),
        compiler_params=pltpu.CompilerParams(dimension_semantics=("parallel",)),
        name="swa_sink_attention",
    )(tb, zn, zn, zn, tb, tb, tb, bias, sinks)


def _t5_bucket(dist):
    n = jnp.maximum(dist, 0)
    max_exact = N_BUCKETS // 2
    nf = jnp.maximum(n, 1).astype(F32)
    large = max_exact + (jnp.log(nf / max_exact) / math.log(MAX_DIST / max_exact)
                         * (N_BUCKETS - max_exact)).astype(jnp.int32)
    large = jnp.minimum(large, N_BUCKETS - 1)
    return jnp.where(n < max_exact, n, large)


def _toeplitz(v, n):
    heads, span = v.shape
    flat = jnp.broadcast_to(v[:, None, :], (heads, n, span)).reshape(heads, n * span)
    return flat[:, :n * (span - 1)].reshape(heads, n, span - 1)[:, :, :n]


def _bias_tiles_a(rel_a):
    t = ROW_TILE
    span = 2 * t
    table = rel_a[_t5_bucket(jnp.arange(span))].astype(F32) - rel_a[N_BUCKETS - 1].astype(F32)
    table = table.T * LOG2E

    def toeplitz(v):
        return _toeplitz(v, t)

    kb = jnp.arange(t)[:, None]
    qa = jnp.arange(t)[None, :]
    pad = (kb < PAD_FRONT)[None]
    causal = jnp.where((qa >= kb)[None], toeplitz(table), NEG)
    corner = toeplitz(jnp.roll(table, -t, axis=1))
    diag = jnp.stack([jnp.where(pad, NEG, causal), causal], axis=1)
    corner = jnp.stack([jnp.full_like(corner, NEG), jnp.where(pad, NEG, corner), corner], axis=1)
    pad_col = jnp.broadcast_to(jnp.where(pad[0], NEG, 0.0).astype(F32), (t, LANES))
    lane = jnp.arange(LANES)[None, :]
    ones_col = (lane == 0)
    kx = jnp.stack([jnp.where(ones_col | ((lane == 1) & pad[0]), 1.0, 0.0),
                    jnp.broadcast_to(jnp.where(ones_col, 1.0, 0.0), (t, LANES))]).astype(BF16)
    return diag, corner, pad_col, kx


def _bias_tiles_b(rel_b):
    span = 4 * BLK
    table = rel_b[_t5_bucket(jnp.arange(span))].astype(F32).T
    kb = jnp.arange(BLK)[:, None]
    qa = jnp.arange(BLK)[None, :]
    is_meta = kb >= PAD_FRONT

    def segment(shift, ok):
        return jnp.where(ok[None], _toeplitz(jnp.roll(table, -shift, axis=1), BLK), NEG)

    prev = segment(BLK, kb > qa)
    cur = segment(0, qa >= kb)
    masked = jnp.full_like(cur, NEG)
    blocks = []
    for n in range(3):
        meta = segment(n * BLK, is_meta & (n * BLK + qa - kb >= 0))
        blocks.append(jnp.concatenate([meta, prev if n >= 2 else masked, cur if n >= 1 else masked], axis=1))
    per_head = jnp.stack(blocks, axis=0)
    pairs = G_B // 2
    stacked = per_head.reshape(3, HBKV, pairs, 2, BAND, BLK).transpose(0, 1, 3, 4, 2, 5)
    return stacked.reshape(3, HBKV, 2 * BAND, PAIR_ROWS)


def _sinks_b(sinks):
    pairs = G_B // 2
    s = sinks.astype(F32).reshape(HBKV, pairs, 2).transpose(0, 2, 1)
    return jnp.repeat(s, BLK, axis=2)[:, :, None, :]


def kernel(x, meta_tokens, rel_bias, w_in, w_branch_a, w_branch_b, w_out, lambda_q1, lambda_k1,
           lambda_q2, lambda_k2, diff_norm, sinks, norm_attn_pre, norm_attn_post, w_ff1, w_ff2,
           norm_ff_pre, norm_ff_post):
    assert x.shape == (1, SEQ, D_MODEL)
    h = jnp.concatenate([jnp.zeros((PAD_FRONT, D_MODEL), F32), meta_tokens.astype(F32), x[0],
                         jnp.zeros((P_PAD - P_STREAM, D_MODEL), F32)], axis=0)
    bdiag, bcorn, bpad, kx = _bias_tiles_a(rel_bias[:, :HA])
    bias_b = _bias_tiles_b(rel_bias[:, HA:HA + HBQ])

    u = _rmsnorm_rows(h, norm_attn_pre[0])
    for l in range(DEPTH):
        lam_init = 0.8 - 0.6 * math.exp(-0.3 * l)
        w = w_in[l]
        wqt = (w[:, :QA_W] * (DHA ** -0.5 * LOG2E)).T.astype(BF16)
        wvt = w[:, 2 * QA_W:3 * QA_W].T.astype(BF16)
        kv_b = w[:, 3 * QA_W + QB_W:GATE_OFF].reshape(D_MODEL, 2 * HBKV, 1, DHB)
        kv_b = jnp.broadcast_to(kv_b, (D_MODEL, 2 * HBKV, 2, DHB)).reshape(D_MODEL, 2, 2 * HBKV * DHB)
        wn = jnp.concatenate([w[:, QA_W:2 * QA_W], kv_b[:, 0]], axis=1).astype(BF16)
        wtb = jnp.concatenate([w[:, 3 * QA_W:3 * QA_W + QB_W] * DHB ** -0.5, kv_b[:, 1]], axis=1).T.astype(BF16)
        qbd, vt, tb, zn = _proj(u, wqt, wvt, wtb, wn)

        lamv = jnp.stack([lambda_q1[l], lambda_k1[l], lambda_q2[l], lambda_k2[l]]).astype(F32)
        a_args = (lamv, diff_norm[l].astype(F32).reshape(2 * DHA, 1), qbd, zn, kx, vt, bdiag, bcorn, bpad, lam_init)
        ya = _attn_a(*a_args, 0, N_TILES - 1, ROW_TILE)
        ya_last = _attn_a(*a_args, N_TILES - 1, 1, LAST_ROWS)
        yb = _attn_b(zn, tb, bias_b, _sinks_b(sinks[l]))

        h, u_ff = _merge(u, ya, ya_last, yb, h, w[:, GATE_OFF:].astype(BF16), w_branch_a[l].astype(BF16),
                         w_branch_b[l].astype(BF16), w_out[l].astype(BF16), norm_attn_post[l], norm_ff_pre[l])
        h, u = _ffn(u_ff, w_ff1[l].astype(BF16), w_ff2[l].astype(BF16), h, norm_ff_post[l],
                    norm_attn_pre[(l + 1) % DEPTH])
    return h[BLK:BLK + SEQ][None]
```

```python
import functools
import math

import jax
import jax.numpy as jnp
from jax import lax
from jax.experimental import pallas as pl
from jax.experimental.pallas import tpu as pltpu

D_MODEL = 2048
SEQ = 16384
DEPTH = 4
N_META = 16
BLK = 128
WINDOW = 128
HA = 8
DHA = 64
HBQ = 16
HBKV = 2
DHB = 64
G_B = HBQ // HBKV
D_FF = 4 * D_MODEL
N_BUCKETS = 32
MAX_DIST = 128
EPS = 1e-6
NEG = -1e30
LOG2E = math.log2(math.e)

QA_W = HA * 2 * DHA
QB_W = HBQ * DHB
GATE_OFF = 3 * QA_W + QB_W + 2 * HBKV * DHB
PAD_FRONT = BLK - N_META
P_STREAM = SEQ + BLK

LANES = 128
BF16_SUBLANES = 16
ROW_TILE = 512
P_PAD = -(-P_STREAM // ROW_TILE) * ROW_TILE
N_TILES = P_PAD // ROW_TILE
N_BLOCKS = P_PAD // BLK
MAIN_ROWS = (N_TILES - 1) * ROW_TILE
LAST_ROWS = P_STREAM - MAIN_ROWS
VT_ROWS = 2 * DHA + BF16_SUBLANES
QX_ROWS = 2 * LANES
PAIR_ROWS = (G_B // 2) * BLK
BAND = 3 * BLK
NAT_W = QA_W + 2 * HBKV * DHB
TB_ROWS = QB_W + 2 * HBKV * DHB
B_BLOCKS_PER_STEP = 4
FAR_GROUP = 8
MERGE_TILE = 256
FF_TILE = 1024
MIB = 1024 * 1024

F32 = jnp.float32
BF16 = jnp.bfloat16
NT_DIMS = (((1,), (1,)), ((), ()))


def _rms(x, g):
    return x * lax.rsqrt(jnp.mean(x * x, axis=-1, keepdims=True) + EPS) * g


def _rmsnorm_kernel(h_ref, g_ref, o_ref):
    o_ref[...] = _rms(h_ref[...], g_ref[...]).astype(o_ref.dtype)


def _rmsnorm_rows(h, g):
    return pl.pallas_call(
        _rmsnorm_kernel,
        out_shape=jax.ShapeDtypeStruct(h.shape, BF16),
        grid=(N_TILES,),
        in_specs=[pl.BlockSpec((ROW_TILE, D_MODEL), lambda i: (i, 0)),
                  pl.BlockSpec((1, D_MODEL), lambda i: (0, 0))],
        out_specs=pl.BlockSpec((ROW_TILE, D_MODEL), lambda i: (i, 0)),
        compiler_params=pltpu.CompilerParams(dimension_semantics=("parallel",)),
        name="rmsnorm_rows",
    )(h, g.reshape(1, D_MODEL))


def _proj_kernel(u_ref, wqt_ref, wvt_ref, wtb_ref, wn_ref, qbd_ref, vt_ref, tb_ref, zn_ref):
    t = ROW_TILE
    u = u_ref[...]
    qt = lax.dot_general(wqt_ref[...], u, NT_DIMS, preferred_element_type=F32).reshape(HA, 2 * DHA, t)
    top = lax.broadcasted_iota(jnp.int32, (1, 2 * DHA, 1), 1) < DHA
    zero = jnp.zeros_like(qt)
    qbd_ref[...] = jnp.concatenate([jnp.where(top, qt, zero), jnp.where(top, zero, qt)], axis=2).astype(BF16)
    vt = lax.dot_general(wvt_ref[...], u, NT_DIMS, preferred_element_type=F32).reshape(HA, 2 * DHA, t)
    vt_ref[:, :2 * DHA, :] = vt.astype(BF16)
    vt_ref[:, 2 * DHA:, :] = jnp.ones((HA, BF16_SUBLANES, t), BF16)
    tb_ref[...] = lax.dot_general(wtb_ref[...], u, NT_DIMS, preferred_element_type=F32).astype(BF16)
    zn_ref[...] = jnp.dot(u, wn_ref[...], preferred_element_type=F32).astype(BF16)


def _proj(u, wqt, wvt, wtb, wn):
    t = ROW_TILE
    once = pl.Buffered(1)
    return pl.pallas_call(
        _proj_kernel,
        out_shape=(jax.ShapeDtypeStruct((HA, 2 * DHA, N_TILES * 2 * t), BF16),
                   jax.ShapeDtypeStruct((HA, N_TILES, VT_ROWS, t), BF16),
                   jax.ShapeDtypeStruct((TB_ROWS, P_PAD), BF16),
                   jax.ShapeDtypeStruct((P_PAD, NAT_W), BF16)),
        grid=(N_TILES,),
        in_specs=[pl.BlockSpec((t, D_MODEL), lambda i: (i, 0)),
                  pl.BlockSpec((QA_W, D_MODEL), lambda i: (0, 0), pipeline_mode=once),
                  pl.BlockSpec((QA_W, D_MODEL), lambda i: (0, 0), pipeline_mode=once),
                  pl.BlockSpec((TB_ROWS, D_MODEL), lambda i: (0, 0), pipeline_mode=once),
                  pl.BlockSpec((D_MODEL, NAT_W), lambda i: (0, 0), pipeline_mode=once)],
        out_specs=(pl.BlockSpec((HA, 2 * DHA, 2 * t), lambda i: (0, 0, i)),
                   pl.BlockSpec((HA, None, VT_ROWS, t), lambda i: (0, i, 0, 0)),
                   pl.BlockSpec((TB_ROWS, t), lambda i: (0, i)),
                   pl.BlockSpec((t, NAT_W), lambda i: (i, 0))),
        compiler_params=pltpu.CompilerParams(
            dimension_semantics=("parallel",), vmem_limit_bytes=56 * MIB),
        name="proj_mixers",
    )(u, wqt, wvt, wtb, wn)


def _merge_kernel(u_ref, ya_ref, ya_last_ref, yb_ref, h_ref, wg_ref, wa_ref, wb_ref, wo_ref,
                  gpost_ref, gnext_ref, h_out_ref, u_out_ref):
    u = u_ref[...]
    ya = jnp.where(pl.program_id(0) >= MAIN_ROWS // MERGE_TILE, ya_last_ref[...], ya_ref[...])
    gate_a = jax.nn.sigmoid(jnp.dot(u, wg_ref[:, :D_MODEL], preferred_element_type=F32))
    mix = gate_a * jnp.dot(ya, wa_ref[...], preferred_element_type=F32)
    gate_b = jax.nn.sigmoid(jnp.dot(u, wg_ref[:, D_MODEL:], preferred_element_type=F32))
    mix = mix + gate_b * jnp.dot(yb_ref[...], wb_ref[...], preferred_element_type=F32)
    t = jnp.dot(mix.astype(BF16), wo_ref[...], preferred_element_type=F32)
    hn = h_ref[...] + _rms(t, gpost_ref[...])
    h_out_ref[...] = hn
    u_out_ref[...] = _rms(hn, gnext_ref[...]).astype(BF16)


def _merge(u, ya, ya_last, yb, h, wg, wa, wb, wo, g_post, g_next):
    tm = MERGE_TILE
    once = pl.Buffered(1)
    main_tiles = MAIN_ROWS // tm
    return pl.pallas_call(
        _merge_kernel,
        out_shape=(jax.ShapeDtypeStruct((P_PAD, D_MODEL), F32),
                   jax.ShapeDtypeStruct((P_PAD, D_MODEL), BF16)),
        grid=(P_PAD // tm,),
        in_specs=[pl.BlockSpec((tm, D_MODEL), lambda i: (i, 0)),
                  pl.BlockSpec((tm, QA_W), lambda i: (jnp.minimum(i, main_tiles - 1), 0)),
                  pl.BlockSpec((tm, QA_W), lambda i: (jnp.maximum(i - main_tiles, 0), 0)),
                  pl.BlockSpec((tm, QB_W), lambda i: (i, 0)),
                  pl.BlockSpec((tm, D_MODEL), lambda i: (i, 0)),
                  pl.BlockSpec((D_MODEL, 2 * D_MODEL), lambda i: (0, 0), pipeline_mode=once),
                  pl.BlockSpec((QA_W, D_MODEL), lambda i: (0, 0), pipeline_mode=once),
                  pl.BlockSpec((QB_W, D_MODEL), lambda i: (0, 0), pipeline_mode=once),
                  pl.BlockSpec((D_MODEL, D_MODEL), lambda i: (0, 0), pipeline_mode=once),
                  pl.BlockSpec((1, D_MODEL), lambda i: (0, 0)),
                  pl.BlockSpec((1, D_MODEL), lambda i: (0, 0))],
        out_specs=(pl.BlockSpec((tm, D_MODEL), lambda i: (i, 0)),
                   pl.BlockSpec((tm, D_MODEL), lambda i: (i, 0))),
        compiler_params=pltpu.CompilerParams(
            dimension_semantics=("parallel",), vmem_limit_bytes=60 * MIB),
        name="gated_merge",
    )(u, ya, ya_last, yb, h, wg, wa, wb, wo, g_post.reshape(1, D_MODEL), g_next.reshape(1, D_MODEL))


def _ffn_kernel(u_ref, w1_ref, w2_ref, h_ref, gpost_ref, gnext_ref, h_out_ref, u_out_ref, acc_ref):
    f = pl.program_id(1)

    @pl.when(f == 0)
    def _():
        acc_ref[...] = jnp.zeros_like(acc_ref)

    a = jnp.dot(u_ref[...], w1_ref[...], preferred_element_type=F32)
    a = jnp.square(jnp.maximum(a, 0.0))
    acc_ref[...] += jnp.dot(a.astype(BF16), w2_ref[...], preferred_element_type=F32)

    @pl.when(f == pl.num_programs(1) - 1)
    def _():
        hn = h_ref[...] + _rms(acc_ref[...], gpost_ref[...])
        h_out_ref[...] = hn
        u_out_ref[...] = _rms(hn, gnext_ref[...]).astype(BF16)


def _ffn(u, w1, w2, h, g_post, g_next):
    tm, tf = ROW_TILE, FF_TILE
    return pl.pallas_call(
        _ffn_kernel,
        out_shape=(jax.ShapeDtypeStruct((P_PAD, D_MODEL), F32),
                   jax.ShapeDtypeStruct((P_PAD, D_MODEL), BF16)),
        grid=(N_TILES, D_FF // tf),
        in_specs=[pl.BlockSpec((tm, D_MODEL), lambda i, f: (i, 0)),
                  pl.BlockSpec((D_MODEL, tf), lambda i, f: (0, f)),
                  pl.BlockSpec((tf, D_MODEL), lambda i, f: (f, 0)),
                  pl.BlockSpec((tm, D_MODEL), lambda i, f: (i, 0)),
                  pl.BlockSpec((1, D_MODEL), lambda i, f: (0, 0)),
                  pl.BlockSpec((1, D_MODEL), lambda i, f: (0, 0))],
        out_specs=(pl.BlockSpec((tm, D_MODEL), lambda i, f: (i, 0)),
                   pl.BlockSpec((tm, D_MODEL), lambda i, f: (i, 0))),
        scratch_shapes=[pltpu.VMEM((tm, D_MODEL), F32)],
        compiler_params=pltpu.CompilerParams(
            dimension_semantics=("parallel", "arbitrary"), vmem_limit_bytes=56 * MIB),
        name="ffn_relu2",
    )(u, w1, w2, h, g_post.reshape(1, D_MODEL), g_next.reshape(1, D_MODEL))


def _attn_a_kernel(lamv_ref, subg_ref, qbd_ref, k_ref, kx_ref, vt_ref, bdiag_ref, bcorn_ref, bpad_ref,
                   o_ref, qx_ref, m_ref, acc_ref, *, lam_init, tq, first_tile):
    t = ROW_TILE
    i = pl.program_id(1) + first_tile
    j_below = jnp.maximum(i - 1, 0)
    n_far = jnp.maximum(i - 1, 0)
    q_op = qbd_ref[...] if tq == t else jnp.concatenate([qbd_ref[:, :tq], qbd_ref[:, t:t + tq]], axis=1)

    def square(b_ref):
        b = b_ref[:, :tq]
        return jnp.concatenate([b, b], axis=1)

    def bias_diag():
        return square(bdiag_ref)

    def bias_below():
        return square(bcorn_ref)

    def keys(j):
        return k_ref[pl.ds(pl.multiple_of(j * t, t), t), :]

    def emit(y):
        o_ref[:tq, :] = y.T.astype(o_ref.dtype)
        if tq < t:
            o_ref[tq:, :] = jnp.zeros((t - tq, 2 * DHA), o_ref.dtype)

    def finish(acc):
        lv = lamv_ref[...]
        lam = (jnp.exp(jnp.sum(lv[0:1] * lv[1:2], axis=-1, keepdims=True))
               - jnp.exp(jnp.sum(lv[2:3] * lv[3:4], axis=-1, keepdims=True)) + lam_init)
        o = acc[:2 * DHA] / acc[2 * DHA:2 * DHA + 1]
        od = o[:, :tq] - lam * o[:, tq:]
        y = od * lax.rsqrt(jnp.mean(od * od, axis=0, keepdims=True) + EPS) * subg_ref[...]
        return y * (1.0 - lam_init)

    s_diag = jnp.dot(keys(i), q_op, preferred_element_type=F32) + bias_diag()
    s_below = jnp.dot(keys(j_below), q_op, preferred_element_type=F32) + bias_below()
    m = jnp.maximum(jnp.max(s_diag, axis=0, keepdims=True), jnp.max(s_below, axis=0, keepdims=True))
    offset_bf = (m + jnp.abs(m) * 2.0 ** -7).astype(BF16)
    offset = offset_bf.astype(F32)
    acc_ref[...] = (jnp.dot(vt_ref[i], jnp.exp2(s_diag - offset).astype(BF16), preferred_element_type=F32)
                    + jnp.dot(vt_ref[j_below], jnp.exp2(s_below - offset).astype(BF16),
                              preferred_element_type=F32))

    qx_ref[:2 * DHA, :] = q_op
    qx_ref[2 * DHA:2 * DHA + BF16_SUBLANES, :] = jnp.concatenate(
        [-offset_bf, jnp.full((1, 2 * tq), NEG, BF16), jnp.zeros((BF16_SUBLANES - 2, 2 * tq), BF16)], axis=0)
    qx_ref[2 * DHA + BF16_SUBLANES:, :] = jnp.zeros((QX_ROWS - 2 * DHA - BF16_SUBLANES, 2 * tq), BF16)

    def far_p(j):
        kx = jnp.concatenate([keys(j), kx_ref[jnp.minimum(j, 1)]], axis=1)
        p = jnp.exp2(jnp.dot(kx, qx_ref[...], preferred_element_type=F32))
        return p.astype(BF16), jnp.sum(p, axis=0, keepdims=True)

    def far_group(width):
        def body(g, first):
            ps = [far_p(first + w) for w in range(width)]
            pv = jnp.dot(vt_ref[first, :2 * DHA, :], ps[0][0], preferred_element_type=F32)
            total = ps[0][1]
            for w in range(1, width):
                pv = pv + jnp.dot(vt_ref[first + w, :2 * DHA, :], ps[w][0], preferred_element_type=F32)
                total = total + ps[w][1]
            acc_ref[:2 * DHA, :] += pv
            acc_ref[2 * DHA:2 * DHA + 1, :] += total
            return first + width
        return body

    done = lax.fori_loop(0, n_far // FAR_GROUP, far_group(FAR_GROUP), jnp.int32(0))
    width = FAR_GROUP // 2
    while width >= 1:
        done = lax.fori_loop(0, (n_far % (2 * width)) // width, far_group(width), done)
        width //= 2

    y = finish(acc_ref[...])
    emit(y)
    all_finite = jnp.min(jnp.where(jnp.abs(y) < jnp.finfo(F32).max, 1.0, 0.0))

    @pl.when(all_finite < 0.5)
    def _():
        m_ref[...] = jnp.full(m_ref.shape, NEG, F32)
        acc_ref[...] = jnp.zeros_like(acc_ref)

        def tile(j, bias):
            s = jnp.dot(keys(j), q_op, preferred_element_type=F32)
            if bias is not None:
                s = s + bias
            m_prev = m_ref[...]
            m_new = jnp.maximum(m_prev, jnp.max(s, axis=0, keepdims=True))
            p = jnp.exp2(s - m_new).astype(BF16)
            acc_ref[...] = (jnp.exp2(m_prev - m_new) * acc_ref[...]
                            + jnp.dot(vt_ref[j], p, preferred_element_type=F32))
            m_ref[...] = m_new

        @pl.when(i >= 2)
        def _():
            tile(0, jnp.concatenate([bpad_ref[...]] * (2 * tq // LANES), axis=1))

        def plain(j, carry):
            tile(j, None)
            return carry

        lax.fori_loop(1, i - 1, plain, 0)

        @pl.when(i >= 1)
        def _():
            tile(i - 1, bias_below())

        tile(i, bias_diag())
        emit(finish(acc_ref[...]))


def _attn_a(lamv, subg, qbd, zn, kx, vt, bdiag, bcorn, bpad, lam_init, first_tile, n_tiles, tq):
    t = ROW_TILE
    f = first_tile
    return pl.pallas_call(
        functools.partial(_attn_a_kernel, lam_init=lam_init, tq=tq, first_tile=f),
        out_shape=jax.ShapeDtypeStruct((n_tiles * t, QA_W), BF16),
        grid=(HA, n_tiles),
        in_specs=[pl.BlockSpec((4, DHA), lambda h, i: (0, 0)),
                  pl.BlockSpec((2 * DHA, 1), lambda h, i: (0, 0)),
                  pl.BlockSpec((None, 2 * DHA, 2 * t), lambda h, i: (h, 0, i + f)),
                  pl.BlockSpec((P_PAD, 2 * DHA), lambda h, i: (0, h)),
                  pl.BlockSpec((2, t, LANES), lambda h, i: (0, 0, 0)),
                  pl.BlockSpec((None, N_TILES, VT_ROWS, t), lambda h, i: (h, 0, 0, 0)),
                  pl.BlockSpec((None, None, t, t), lambda h, i: (h, jnp.minimum(i + f, 1), 0, 0)),
                  pl.BlockSpec((None, None, t, t), lambda h, i: (h, jnp.minimum(i + f, 2), 0, 0)),
                  pl.BlockSpec((t, LANES), lambda h, i: (0, 0))],
        out_specs=pl.BlockSpec((t, 2 * DHA), lambda h, i: (i, h)),
        scratch_shapes=[pltpu.VMEM((QX_ROWS, 2 * tq), BF16),
                        pltpu.VMEM((1, 2 * tq), F32),
                        pltpu.VMEM((VT_ROWS, 2 * tq), F32)],
        compiler_params=pltpu.CompilerParams(
            dimension_semantics=("parallel", "arbitrary"), vmem_limit_bytes=56 * MIB),
        name="diff_attention" if tq == t else "diff_attention_last",
    )(lamv, subg, qbd, zn, kx, vt, bdiag, bcorn, bpad)


def _attn_b_kernel(q_ref, km_ref, kp_ref, kc_ref, vm_ref, vp_ref, vc_ref, bias_ref, sink_ref, o_ref):
    pairs = G_B // 2
    first = lax.broadcasted_iota(jnp.int32, (1, LANES), 1) < DHB
    top = lax.broadcasted_iota(jnp.int32, (LANES, 1), 0) < DHB
    row = lax.broadcasted_iota(jnp.int32, (BF16_SUBLANES, 2 * BAND), 0)
    col = lax.broadcasted_iota(jnp.int32, (BF16_SUBLANES, 2 * BAND), 1)
    ones = jnp.where(row == col // BAND, 1.0, 0.0).astype(BF16)

    for b in range(B_BLOCKS_PER_STEP):
        rows = slice(BLK * b, BLK * (b + 1))
        before = slice(BLK * (b - 1), BLK * b)
        variant = jnp.minimum(pl.program_id(0) * B_BLOCKS_PER_STEP + b, 2)
        for g in range(HBKV):
            lanes = slice(LANES * g, LANES * (g + 1))
            qt = q_ref[PAIR_ROWS * g:PAIR_ROWS * (g + 1), rows]
            rhs = jnp.concatenate([qt[LANES * p:LANES * (p + 1)] for p in range(pairs)], axis=1)
            k_prev = kp_ref[:, lanes] if b == 0 else kc_ref[before, lanes]
            kk = jnp.concatenate([km_ref[:, lanes], k_prev, kc_ref[rows, lanes]], axis=0)
            kk = jnp.concatenate([jnp.where(first, kk, jnp.zeros_like(kk)),
                                  jnp.where(first, jnp.zeros_like(kk), kk)], axis=0)
            s = jnp.dot(kk, rhs, preferred_element_type=F32) + bias_ref[variant, g]

            v_prev = vp_ref[lanes, :] if b == 0 else vc_ref[lanes, before]
            vt = jnp.concatenate([vm_ref[lanes, :], v_prev, vc_ref[lanes, rows]], axis=1)
            vt = jnp.concatenate([jnp.where(top, vt, jnp.zeros_like(vt)),
                                  jnp.where(top, jnp.zeros_like(vt), vt)], axis=1)

            probs, maxes = [], []
            for e in range(2):
                se = s[e * BAND:(e + 1) * BAND]
                m = jnp.maximum(jnp.max(se, axis=0, keepdims=True), sink_ref[g, e])
                probs.append(jnp.exp(se - m).astype(BF16))
                maxes.append(m)
            acc = jnp.dot(jnp.concatenate([vt, ones], axis=0), jnp.concatenate(probs, axis=0),
                          preferred_element_type=F32)
            denom = [acc[2 * DHB + e:2 * DHB + e + 1] + jnp.exp(sink_ref[g, e] - maxes[e]) for e in range(2)]
            o = acc[:2 * DHB] / jnp.where(top, denom[0], denom[1])
            o_ref[rows, PAIR_ROWS * g:PAIR_ROWS * (g + 1)] = jnp.concatenate(
                [o[:, BLK * p:BLK * (p + 1)].T for p in range(pairs)], axis=1).astype(o_ref.dtype)


def _attn_b(zn, tb, bias, sinks):
    nb = B_BLOCKS_PER_STEP
    kv_w = 2 * HBKV * DHB
    k_blk = QA_W // kv_w
    v_blk = QB_W // kv_w
    once = pl.Buffered(1)

    def before(s):
        return jnp.maximum(s * nb - 1, 0)

    return pl.pallas_call(
        _attn_b_kernel,
        out_shape=jax.ShapeDtypeStruct((P_PAD, QB_W), BF16),
        grid=(N_BLOCKS // nb,),
        in_specs=[pl.BlockSpec((QB_W, nb * BLK), lambda s: (0, s)),
                  pl.BlockSpec((BLK, kv_w), lambda s: (0, k_blk)),
                  pl.BlockSpec((BLK, kv_w), lambda s: (before(s), k_blk)),
                  pl.BlockSpec((nb * BLK, kv_w), lambda s: (s, k_blk)),
                  pl.BlockSpec((kv_w, BLK), lambda s: (v_blk, 0)),
                  pl.BlockSpec((kv_w, BLK), lambda s: (v_blk, before(s))),
                  pl.BlockSpec((kv_w, nb * BLK), lambda s: (v_blk, s)),
                  pl.BlockSpec((3, HBKV, 2 * BAND, PAIR_ROWS), lambda s: (0, 0, 0, 0), pipeline_mode=once),
                  pl.BlockSpec((HBKV, 2, 1, PAIR_ROWS), lambda s: (0, 0, 0, 0))],
        out_specs=pl.BlockSpec((nb * BLK, QB_W), lambda s: (s, 0)),
        compiler_params=pltpu.CompilerParams(dimension_semantics=("parallel",)),
        name="swa_sink_attention",
    )(tb, zn, zn, zn, tb, tb, tb, bias, sinks)


def _t5_bucket(dist):
    n = jnp.maximum(dist, 0)
    max_exact = N_BUCKETS // 2
    nf = jnp.maximum(n, 1).astype(F32)
    large = max_exact + (jnp.log(nf / max_exact) / math.log(MAX_DIST / max_exact)
                         * (N_BUCKETS - max_exact)).astype(jnp.int32)
    large = jnp.minimum(large, N_BUCKETS - 1)
    return jnp.where(n < max_exact, n, large)


def _toeplitz(v, n):
    heads, span = v.shape
    flat = jnp.broadcast_to(v[:, None, :], (heads, n, span)).reshape(heads, n * span)
    return flat[:, :n * (span - 1)].reshape(heads, n, span - 1)[:, :, :n]


def _bias_tiles_a(rel_a):
    t = ROW_TILE
    span = 2 * t
    table = rel_a[_t5_bucket(jnp.arange(span))].astype(F32) - rel_a[N_BUCKETS - 1].astype(F32)
    table = table.T * LOG2E

    def toeplitz(v):
        return _toeplitz(v, t)

    kb = jnp.arange(t)[:, None]
    qa = jnp.arange(t)[None, :]
    pad = (kb < PAD_FRONT)[None]
    causal = jnp.where((qa >= kb)[None], toeplitz(table), NEG)
    corner = toeplitz(jnp.roll(table, -t, axis=1))
    diag = jnp.stack([jnp.where(pad, NEG, causal), causal], axis=1)
    corner = jnp.stack([jnp.full_like(corner, NEG), jnp.where(pad, NEG, corner), corner], axis=1)
    pad_col = jnp.broadcast_to(jnp.where(pad[0], NEG, 0.0).astype(F32), (t, LANES))
    lane = jnp.arange(LANES)[None, :]
    ones_col = (lane == 0)
    kx = jnp.stack([jnp.where(ones_col | ((lane == 1) & pad[0]), 1.0, 0.0),
                    jnp.broadcast_to(jnp.where(ones_col, 1.0, 0.0), (t, LANES))]).astype(BF16)
    return diag, corner, pad_col, kx


def _bias_tiles_b(rel_b):
    span = 4 * BLK
    table = rel_b[_t5_bucket(jnp.arange(span))].astype(F32).T
    kb = jnp.arange(BLK)[:, None]
    qa = jnp.arange(BLK)[None, :]
    is_meta = kb >= PAD_FRONT

    def segment(shift, ok):
        return jnp.where(ok[None], _toeplitz(jnp.roll(table, -shift, axis=1), BLK), NEG)

    prev = segment(BLK, kb > qa)
    cur = segment(0, qa >= kb)
    masked = jnp.full_like(cur, NEG)
    blocks = []
    for n in range(3):
        meta = segment(n * BLK, is_meta & (n * BLK + qa - kb >= 0))
        blocks.append(jnp.concatenate([meta, prev if n >= 2 else masked, cur if n >= 1 else masked], axis=1))
    per_head = jnp.stack(blocks, axis=0)
    pairs = G_B // 2
    stacked = per_head.reshape(3, HBKV, pairs, 2, BAND, BLK).transpose(0, 1, 3, 4, 2, 5)
    return stacked.reshape(3, HBKV, 2 * BAND, PAIR_ROWS)


def _sinks_b(sinks):
    pairs = G_B // 2
    s = sinks.astype(F32).reshape(HBKV, pairs, 2).transpose(0, 2, 1)
    return jnp.repeat(s, BLK, axis=2)[:, :, None, :]


def kernel(x, meta_tokens, rel_bias, w_in, w_branch_a, w_branch_b, w_out, lambda_q1, lambda_k1,
           lambda_q2, lambda_k2, diff_norm, sinks, norm_attn_pre, norm_attn_post, w_ff1, w_ff2,
           norm_ff_pre, norm_ff_post):
    assert x.shape == (1, SEQ, D_MODEL)
    h = jnp.concatenate([jnp.zeros((PAD_FRONT, D_MODEL), F32), meta_tokens.astype(F32), x[0],
                         jnp.zeros((P_PAD - P_STREAM, D_MODEL), F32)], axis=0)
    bdiag, bcorn, bpad, kx = _bias_tiles_a(rel_bias[:, :HA])
    bias_b = _bias_tiles_b(rel_bias[:, HA:HA + HBQ])

    u = _rmsnorm_rows(h, norm_attn_pre[0])
    for l in range(DEPTH):
        lam_init = 0.8 - 0.6 * math.exp(-0.3 * l)
        w = w_in[l]
        wqt = (w[:, :QA_W] * (DHA ** -0.5 * LOG2E)).T.astype(BF16)
        wvt = w[:, 2 * QA_W:3 * QA_W].T.astype(BF16)
        kv_b = w[:, 3 * QA_W + QB_W:GATE_OFF].reshape(D_MODEL, 2 * HBKV, 1, DHB)
        kv_b = jnp.broadcast_to(kv_b, (D_MODEL, 2 * HBKV, 2, DHB)).reshape(D_MODEL, 2, 2 * HBKV * DHB)
        wn = jnp.concatenate([w[:, QA_W:2 * QA_W], kv_b[:, 0]], axis=1).astype(BF16)
        wtb = jnp.concatenate([w[:, 3 * QA_W:3 * QA_W + QB_W] * DHB ** -0.5, kv_b[:, 1]], axis=1).T.astype(BF16)
        qbd, vt, tb, zn = _proj(u, wqt, wvt, wtb, wn)

        lamv = jnp.stack([lambda_q1[l], lambda_k1[l], lambda_q2[l], lambda_k2[l]]).astype(F32)
        a_args = (lamv, diff_norm[l].astype(F32).reshape(2 * DHA, 1), qbd, zn, kx, vt, bdiag, bcorn, bpad, lam_init)
        ya = _attn_a(*a_args, 0, N_TILES - 1, ROW_TILE)
        ya_last = _attn_a(*a_args, N_TILES - 1, 1, LAST_ROWS)
        yb = _attn_b(zn, tb, bias_b, _sinks_b(sinks[l]))

        h, u_ff = _merge(u, ya, ya_last, yb, h, w[:, GATE_OFF:].astype(BF16), w_branch_a[l].astype(BF16),
                         w_branch_b[l].astype(BF16), w_out[l].astype(BF16), norm_attn_post[l], norm_ff_pre[l])
        h, u = _ffn(u_ff, w_ff1[l].astype(BF16), w_ff2[l].astype(BF16), h, norm_ff_post[l],
                    norm_attn_pre[(l + 1) % DEPTH])
    return h[BLK:BLK + SEQ][None]
```
